```python
import math
import jax, jax.numpy as jnp
from jax import lax
import numpy as np

D_MODEL = 1024
BATCH = 16
SEQ = 2048
DEPTH = 4

GRID_W = 64
CTX_LEN = 256
N_HEADS = 8
N_KV_HEADS = 2
HEAD_DIM = 128
ATTN_WIDTH = N_HEADS * HEAD_DIM
KV_WIDTH = N_KV_HEADS * HEAD_DIM
WINDOW = 128
BLOCK = 128
ROPE_THETA = 10000.0
D_RNN = 1024
RNN_BLOCKS = 16
RNN_BLOCK_W = D_RNN // RNN_BLOCKS
CONV_W = 4
LRU_C = 8.0
N_EXPERTS = 64
TOP_K = 8
N_GROUPS = 8
TOPK_GROUPS = 4
D_EXPERT = 256
D_SHARED = 256
ROUTE_SCALE = 2.5
LN_EPS = 1e-6
ALPHA = (2 * DEPTH) ** 0.25
BETA = (8 * DEPTH) ** -0.25
IN_SPLITS = (ATTN_WIDTH, KV_WIDTH, KV_WIDTH, D_RNN, D_RNN, D_MODEL, D_MODEL)
IN_WIDTH = sum(IN_SPLITS)
IN_OFFSETS = tuple(int(o) for o in np.cumsum(IN_SPLITS)[:-1])

kernel_name = "hybrid_swa_rglru_moe_diffusion_trunk"


def layer_norm(x, g, b):
    xf = x.astype(jnp.float32)
    mu = xf.mean(-1, keepdims=True)
    var = jnp.square(xf - mu).mean(-1, keepdims=True)
    return ((xf - mu) * lax.rsqrt(var + LN_EPS) * g + b).astype(x.dtype)


def axial_rope(n_tokens):
    rows = n_tokens // GRID_W
    row = jnp.repeat(jnp.arange(rows), GRID_W).astype(jnp.float32)
    col = jnp.tile(jnp.arange(GRID_W), rows).astype(jnp.float32)
    n_freq = HEAD_DIM // 4
    inv = ROPE_THETA ** (-jnp.arange(n_freq, dtype=jnp.float32) / n_freq)
    ang = jnp.stack([row[:, None] * inv, col[:, None] * inv], axis=1)
    return jnp.cos(ang), jnp.sin(ang)


def apply_rope(x, cos, sin):
    shp = x.shape
    xs = x.reshape(shp[:-1] + (2, 2, HEAD_DIM // 4)).astype(jnp.float32)
    x1, x2 = xs[..., 0, :], xs[..., 1, :]
    cs, sn = cos[None, :, None], sin[None, :, None]
    out = jnp.stack([x1 * cs - x2 * sn, x1 * sn + x2 * cs], axis=-2)
    return out.reshape(shp).astype(x.dtype)


def window_ctx_attention(q, k, v, k_ctx, v_ctx, sink):
    B, S = q.shape[:2]
    nb = S // BLOCK
    G = N_HEADS // N_KV_HEADS
    qb = q.reshape(B, nb, BLOCK, N_KV_HEADS, G, HEAD_DIM)
    pad = ((0, 0), (BLOCK, BLOCK), (0, 0), (0, 0))
    kp = jnp.pad(k, pad).reshape(B, nb + 2, BLOCK, N_KV_HEADS, HEAD_DIM)
    vp = jnp.pad(v, pad).reshape(B, nb + 2, BLOCK, N_KV_HEADS, HEAD_DIM)
    kw = jnp.concatenate([kp[:, :-2], kp[:, 1:-1], kp[:, 2:]], axis=2)
    vw = jnp.concatenate([vp[:, :-2], vp[:, 1:-1], vp[:, 2:]], axis=2)
    scale = HEAD_DIM ** -0.5
    s_win = jnp.einsum('bnqkgd,bnjkd->bnkgqj', qb, kw).astype(jnp.float32) * scale
    s_ctx = jnp.einsum('bnqkgd,bckd->bnkgqc', qb, k_ctx).astype(jnp.float32) * scale
    qpos = jnp.arange(nb)[:, None, None] * BLOCK + jnp.arange(BLOCK)[None, :, None]
    kpos = (jnp.arange(nb)[:, None, None] - 1) * BLOCK + jnp.arange(3 * BLOCK)[None, None, :]
    allowed = (jnp.abs(qpos - kpos) <= WINDOW) & (kpos >= 0) & (kpos < S)
    s_win = jnp.where(allowed[None, :, None, None], s_win, -jnp.inf)
    sink_l = jnp.broadcast_to(sink.astype(jnp.float32).reshape(N_KV_HEADS, G)[None, None, :, :, None, None],
                              s_win.shape[:-1] + (1,))
    p = jax.nn.softmax(jnp.concatenate([sink_l, s_win, s_ctx], axis=-1), axis=-1)
    p_win = p[..., 1:1 + 3 * BLOCK].astype(v.dtype)
    p_ctx = p[..., 1 + 3 * BLOCK:].astype(v.dtype)
    o = (jnp.einsum('bnkgqj,bnjkd->bnqkgd', p_win, vw)
         + jnp.einsum('bnkgqc,bckd->bnqkgd', p_ctx, v_ctx))
    return o.reshape(B, S, ATTN_WIDTH)


def ctx_attention(q, k, v, sink):
    B, C = q.shape[:2]
    G = N_HEADS // N_KV_HEADS
    qg = q.reshape(B, C, N_KV_HEADS, G, HEAD_DIM)
    s = jnp.einsum('bqkgd,bckd->bkgqc', qg, k).astype(jnp.float32) * HEAD_DIM ** -0.5
    sink_l = jnp.broadcast_to(sink.astype(jnp.float32).reshape(N_KV_HEADS, G)[None, :, :, None, None],
                              s.shape[:-1] + (1,))
    p = jax.nn.softmax(jnp.concatenate([sink_l, s], axis=-1), axis=-1)[..., 1:].astype(v.dtype)
    return jnp.einsum('bkgqc,bckd->bqkgd', p, v).reshape(B, C, ATTN_WIDTH)


def block_diag(x, w, b):
    xb = x.reshape(x.shape[:-1] + (RNN_BLOCKS, RNN_BLOCK_W))
    return jnp.einsum('btnd,nde->btne', xb, w).reshape(x.shape) + b


def centred_conv(x, w, b):
    T = x.shape[1]
    left = CONV_W // 2
    xp = jnp.pad(x, ((0, 0), (left, CONV_W - 1 - left), (0, 0)))
    out = b
    for tap in range(CONV_W):
        out = out + xp[:, tap:tap + T] * w[tap]
    return out


def _lin_combine(left, right):
    a_l, b_l = left
    a_r, b_r = right
    return a_l * a_r, a_r * b_l + b_r


def rglru_scan(x, w_a, b_a, w_i, b_i, lam, h0, reverse):
    xf = x.astype(jnp.float32)
    if reverse:
        xf = jnp.flip(xf, axis=1)
    r = jax.nn.sigmoid(block_diag(xf, w_a, b_a))
    i = jax.nn.sigmoid(block_diag(xf, w_i, b_i))
    log_a = -LRU_C * r * jax.nn.softplus(-lam)
    a = jnp.exp(log_a)
    b = jnp.sqrt(-jnp.expm1(2.0 * log_a)) * (i * xf)
    a_cum, h = lax.associative_scan(_lin_combine, (a, b), axis=1)
    h = h + a_cum * h0[:, None, :]
    final = h[:, -1]
    if reverse:
        h = jnp.flip(h, axis=1)
    return h.astype(x.dtype), final


def token_mixer(u, u_ctx, w_in, sink, conv_w, conv_b, rg_wa, rg_ba, rg_wi, rg_bi, rg_lambda,
                w_o_attn, w_o_rnn, w_out, cos, sin, with_ctx_out):
    B, S = u.shape[:2]
    C = u_ctx.shape[1]
    q, k, v, xr, yr, ga, gr = jnp.split(u @ w_in, IN_OFFSETS, axis=-1)
    qc, kc, vc, xrc, yrc, gac, grc = jnp.split(u_ctx @ w_in, IN_OFFSETS, axis=-1)

    q = apply_rope(q.reshape(B, S, N_HEADS, HEAD_DIM), cos, sin)
    k = apply_rope(k.reshape(B, S, N_KV_HEADS, HEAD_DIM), cos, sin)
    v = v.reshape(B, S, N_KV_HEADS, HEAD_DIM)
    kc = kc.reshape(B, C, N_KV_HEADS, HEAD_DIM)
    vc = vc.reshape(B, C, N_KV_HEADS, HEAD_DIM)
    attn = window_ctx_attention(q, k, v, kc, vc, sink)

    x_lat = centred_conv(xr, conv_w, conv_b)
    x_ctx = centred_conv(xrc, conv_w, conv_b)
    h0 = jnp.zeros((B, D_RNN), jnp.float32)
    hc_f, fin_f = rglru_scan(x_ctx, rg_wa[0], rg_ba[0], rg_wi[0], rg_bi[0], rg_lambda[0], h0, False)
    hc_b, fin_b = rglru_scan(x_ctx, rg_wa[1], rg_ba[1], rg_wi[1], rg_bi[1], rg_lambda[1], h0, True)
    hl_f, _ = rglru_scan(x_lat, rg_wa[0], rg_ba[0], rg_wi[0], rg_bi[0], rg_lambda[0], fin_f, False)
    hl_b, _ = rglru_scan(x_lat, rg_wa[1], rg_ba[1], rg_wi[1], rg_bi[1], rg_lambda[1], fin_b, True)
    rnn = (hl_f + hl_b) * jax.nn.gelu(yr)

    merged = jax.nn.sigmoid(ga) * (attn @ w_o_attn) + jax.nn.sigmoid(gr) * (rnn @ w_o_rnn)
    out = merged @ w_out
    if not with_ctx_out:
        return out, None
    attn_c = ctx_attention(qc.reshape(B, C, N_HEADS, HEAD_DIM), kc, vc, sink)
    rnn_c = (hc_f + hc_b) * jax.nn.gelu(yrc)
    merged_c = jax.nn.sigmoid(gac) * (attn_c @ w_o_attn) + jax.nn.sigmoid(grc) * (rnn_c @ w_o_rnn)
    return out, merged_c @ w_out


def route(tok, router_w, router_bias):
    scores = jax.nn.sigmoid((tok @ router_w).astype(jnp.float32))
    biased = scores + router_bias.astype(jnp.float32)
    per_group = N_EXPERTS // N_GROUPS
    grp_score = lax.top_k(biased.reshape(-1, N_GROUPS, per_group), 2)[0].sum(-1)
    _, grp_idx = lax.top_k(grp_score, TOPK_GROUPS)
    grp_mask = jax.nn.one_hot(grp_idx, N_GROUPS, dtype=jnp.float32).sum(1)
    expert_mask = jnp.repeat(grp_mask, per_group, axis=-1) > 0
    _, idx = lax.top_k(jnp.where(expert_mask, biased, -jnp.inf), TOP_K)
    w = jnp.take_along_axis(scores, idx, axis=-1)
    w = w / w.sum(-1, keepdims=True) * ROUTE_SCALE
    return jnp.einsum('tk,tke->te', w, jax.nn.one_hot(idx, N_EXPERTS, dtype=jnp.float32))


def moe_ffn(tok, router_w, router_bias, w1, w3, w2, sw1, sw3, sw2):
    combine = route(tok, router_w, router_bias)

    def expert_step(acc, params):
        e_w1, e_w3, e_w2, gate = params
        h = jax.nn.silu(tok @ e_w1) * (tok @ e_w3)
        return acc + gate[:, None].astype(tok.dtype) * (h @ e_w2), None

    routed, _ = lax.scan(expert_step, jnp.zeros_like(tok), (w1, w3, w2, combine.T))
    shared = (jax.nn.silu(tok @ sw1) * (tok @ sw3)) @ sw2
    return routed + shared


def setup_inputs(seed: int = 0) -> dict:
    key = jax.random.key(seed)
    ks = jax.random.split(key, 32)
    f32 = jnp.float32
    L = DEPTH

    def nrm(k, shape, scale):
        return jax.random.normal(k, shape, f32) * scale

    a_init = jax.random.uniform(ks[14], (L, 2, D_RNN), f32, 0.9, 0.999)
    s_init = a_init ** (1.0 / LRU_C)
    rg_lambda = jnp.log(s_init) - jnp.log1p(-s_init)
    return {
        "x": nrm(ks[0], (BATCH, SEQ, D_MODEL), 1.0),
        "c": nrm(ks[1], (BATCH, D_MODEL), 1.0),
        "ctx": nrm(ks[2], (BATCH, CTX_LEN, D_MODEL), 1.0),
        "c_ctx": nrm(ks[3], (D_MODEL,), 1.0),
        "w_mod": nrm(ks[4], (L, D_MODEL, 6 * D_MODEL), 0.5 * D_MODEL ** -0.5),
        "b_mod": nrm(ks[5], (L, 6 * D_MODEL), 0.02),
        "w_in": nrm(ks[6], (L, D_MODEL, IN_WIDTH), D_MODEL ** -0.5),
        "sink": nrm(ks[7], (L, N_HEADS), 0.5),
        "conv_w": nrm(ks[8], (L, CONV_W, D_RNN), CONV_W ** -0.5),
        "conv_b": nrm(ks[9], (L, D_RNN), 0.02),
        "rg_wa": nrm(ks[10], (L, 2, RNN_BLOCKS, RNN_BLOCK_W, RNN_BLOCK_W), RNN_BLOCK_W ** -0.5),
        "rg_ba": nrm(ks[11], (L, 2, D_RNN), 0.02),
        "rg_wi": nrm(ks[12], (L, 2, RNN_BLOCKS, RNN_BLOCK_W, RNN_BLOCK_W), RNN_BLOCK_W ** -0.5),
        "rg_bi": nrm(ks[13], (L, 2, D_RNN), 0.02),
        "rg_lambda": rg_lambda,
        "w_o_attn": nrm(ks[15], (L, ATTN_WIDTH, D_MODEL), ATTN_WIDTH ** -0.5),
        "w_o_rnn": nrm(ks[16], (L, D_RNN, D_MODEL), D_RNN ** -0.5),
        "w_out": nrm(ks[17], (L, D_MODEL, D_MODEL), BETA * D_MODEL ** -0.5),
        "ln1_g": 1.0 + nrm(ks[18], (L, D_MODEL), 0.02),
        "ln1_b": nrm(ks[19], (L, D_MODEL), 0.02),
        "router_w": nrm(ks[20], (L, D_MODEL, N_EXPERTS), D_MODEL ** -0.5),
        "router_bias": nrm(ks[21], (L, N_EXPERTS), 0.01),
        "exp_w1": nrm(ks[22], (L, N_EXPERTS, D_MODEL, D_EXPERT), D_MODEL ** -0.5),
        "exp_w3": nrm(ks[23], (L, N_EXPERTS, D_MODEL, D_EXPERT), D_MODEL ** -0.5),
        "exp_w2": nrm(ks[24], (L, N_EXPERTS, D_EXPERT, D_MODEL), BETA * D_EXPERT ** -0.5),
        "sh_w1": nrm(ks[25], (L, D_MODEL, D_SHARED), D_MODEL ** -0.5),
        "sh_w3": nrm(ks[26], (L, D_MODEL, D_SHARED), D_MODEL ** -0.5),
        "sh_w2": nrm(ks[27], (L, D_SHARED, D_MODEL), BETA * D_SHARED ** -0.5),
        "ln2_g": 1.0 + nrm(ks[28], (L, D_MODEL), 0.02),
        "ln2_b": nrm(ks[29], (L, D_MODEL), 0.02),
    }


def reference(x, c, ctx, c_ctx, w_mod, b_mod, w_in, sink, conv_w, conv_b, rg_wa, rg_ba, rg_wi, rg_bi,
              rg_lambda, w_o_attn, w_o_rnn, w_out, ln1_g, ln1_b, router_w, router_bias, exp_w1, exp_w3,
              exp_w2, sh_w1, sh_w3, sh_w2, ln2_g, ln2_b):
    B, S, _ = x.shape
    cos, sin = axial_rope(S)
    n_lat = B * S
    for l in range(DEPTH):
        last = l == DEPTH - 1
        mod = (jax.nn.silu(c) @ w_mod[l] + b_mod[l])[:, None, :]
        mod_ctx = jax.nn.silu(c_ctx) @ w_mod[l] + b_mod[l]
        sh1, sc1, g1, sh2, sc2, g2 = jnp.split(mod, 6, axis=-1)
        csh1, csc1, cg1, csh2, csc2, cg2 = jnp.split(mod_ctx, 6, axis=-1)

        mix, mix_ctx = token_mixer(x * (1.0 + sc1) + sh1, ctx * (1.0 + csc1) + csh1,
                                   w_in[l], sink[l], conv_w[l], conv_b[l], rg_wa[l], rg_ba[l],
                                   rg_wi[l], rg_bi[l], rg_lambda[l], w_o_attn[l], w_o_rnn[l], w_out[l],
                                   cos, sin, not last)
        x = layer_norm(ALPHA * x + g1 * mix, ln1_g[l], ln1_b[l])

        u2 = (x * (1.0 + sc2) + sh2).reshape(n_lat, D_MODEL)
        moe_args = (router_w[l], router_bias[l], exp_w1[l], exp_w3[l], exp_w2[l],
                    sh_w1[l], sh_w3[l], sh_w2[l])
        if last:
            ffn = moe_ffn(u2, *moe_args)
        else:
            ctx = layer_norm(ALPHA * ctx + cg1 * mix_ctx, ln1_g[l], ln1_b[l])
            u2c = (ctx * (1.0 + csc2) + csh2).reshape(-1, D_MODEL)
            both = moe_ffn(jnp.concatenate([u2, u2c], axis=0), *moe_args)
            ffn = both[:n_lat]
            ctx = layer_norm(ALPHA * ctx + cg2 * both[n_lat:].reshape(ctx.shape), ln2_g[l], ln2_b[l])
        x = layer_norm(ALPHA * x + g2 * ffn.reshape(x.shape), ln2_g[l], ln2_b[l])
    return x
```

```python
import functools
import math

import jax
import jax.numpy as jnp
from jax import lax
from jax.experimental import pallas as pl
from jax.experimental.pallas import tpu as pltpu

N_HEADS = 8
N_KV_HEADS = 2
HEAD_DIM = 128
KV_GROUP = N_HEADS // N_KV_HEADS
ATT_BLOCK = 128
GRID_W = 64
ROPE_THETA = 10000.0
RNN_BLOCKS = 16
LRU_C = 8.0
N_EXPERTS = 64
TOP_K = 8
N_GROUPS = 8
TOPK_GROUPS = 4
D_EXPERT = 256
ROUTE_SCALE = 2.5
LN_EPS = 1e-6

ROW_TILE = 256
MXU_TILE = 256
MOE_ROW_TILE = 1024
MOD_ROWS = 24
VMEM_LIMIT = 56 * 1024 * 1024

F32 = jnp.float32
BF16 = jnp.bfloat16
NEG_BIG = -1e30


def _const_spec(shape):
    zeros = (0,) * len(shape)
    return pl.BlockSpec(shape, lambda *_: zeros, pipeline_mode=pl.Buffered(1))


def _params(sem):
    return pltpu.CompilerParams(dimension_semantics=sem, vmem_limit_bytes=VMEM_LIMIT)


def _silu(v):
    return v * jax.nn.sigmoid(v)


def _gelu_tanh(v):
    return v * (0.5 * (1.0 + jnp.tanh(math.sqrt(2.0 / math.pi) * (v + 0.044715 * (v * v * v)))))


def _layer_norm(v, g, b):
    mu = jnp.mean(v, axis=-1, keepdims=True)
    d = v - mu
    var = jnp.mean(d * d, axis=-1, keepdims=True)
    return d * lax.rsqrt(var + LN_EPS) * g + b


def _mod_kernel(cc_ref, w_ref, b_ref, o_ref):
    s = _silu(cc_ref[...]).astype(BF16)
    o_ref[0] = jnp.dot(s, w_ref[0].astype(BF16), preferred_element_type=F32) + b_ref[0]


def _modulation(cc, w_mod, b_mod):
    n_layers, d, d6 = w_mod.shape
    col = 1536
    return pl.pallas_call(
        _mod_kernel,
        out_shape=jax.ShapeDtypeStruct((n_layers, MOD_ROWS, d6), F32),
        grid=(n_layers, d6 // col),
        in_specs=[
            pl.BlockSpec((MOD_ROWS, d), lambda l, j: (0, 0)),
            pl.BlockSpec((1, d, col), lambda l, j: (l, 0, j)),
            pl.BlockSpec((1, 1, col), lambda l, j: (l, 0, j)),
        ],
        out_specs=pl.BlockSpec((1, MOD_ROWS, col), lambda l, j: (l, 0, j)),
        compiler_params=_params(("parallel", "parallel")),
        name="modulation",
    )(cc, w_mod, b_mod.reshape(n_layers, 1, d6))


def _mod_row(mod_ref, n_batch, n_ctx_tiles):
    b = pl.program_id(0)
    j = pl.program_id(1)
    row = jnp.where(j < n_ctx_tiles, n_batch, b)
    return mod_ref[pl.ds(row, 1), :]


def _rope(v, cos_w, sin_w):
    width = v.shape[1]
    lane = lax.broadcasted_iota(jnp.int32, v.shape, 1)
    partner = jnp.where((lane & 32) == 0, pltpu.roll(v, width - 32, 1), pltpu.roll(v, 32, 1))
    return v * cos_w + partner * sin_w


def _inproj_kernel(x_ref, mod_ref, w_ref, cos_ref, sin_ref,
                   q_ref, k_ref, v_ref, xr_ref, gy_ref, sa_ref, sr_ref,
                   *, n_batch, n_ctx_tiles, d):
    m = _mod_row(mod_ref, n_batch, n_ctx_tiles)
    sh1 = m[:, 0:d]
    sc1 = m[:, d:2 * d]
    u = (x_ref[0] * (1.0 + sc1) + sh1).astype(BF16)

    def proj(lo, hi):
        return jnp.dot(u, w_ref[:, lo:hi], preferred_element_type=F32)

    qw = N_HEADS * HEAD_DIM
    kw = N_KV_HEADS * HEAD_DIM
    cos1 = cos_ref[...]
    sin1 = sin_ref[...]
    o = 0
    q = proj(o, o + qw)
    q_ref[0] = _rope(q, jnp.concatenate([cos1] * N_HEADS, axis=1),
                     jnp.concatenate([sin1] * N_HEADS, axis=1)).astype(BF16)
    o += qw
    k = proj(o, o + kw)
    k_ref[0] = _rope(k, jnp.concatenate([cos1] * N_KV_HEADS, axis=1),
                     jnp.concatenate([sin1] * N_KV_HEADS, axis=1)).astype(BF16)
    o += kw
    v_ref[0] = proj(o, o + kw).astype(BF16)
    o += kw
    xr_ref[0] = proj(o, o + d)
    o += d
    gy_ref[0] = _gelu_tanh(proj(o, o + d)).astype(BF16)
    o += d
    sa_ref[0] = jax.nn.sigmoid(proj(o, o + d)).astype(BF16)
    o += d
    sr_ref[0] = jax.nn.sigmoid(proj(o, o + d)).astype(BF16)


def _in_projection(xa, mod_l, w_in, cos_t, sin_t, n_ctx):
    n_batch, t, d = xa.shape
    qw = N_HEADS * HEAD_DIM
    kw = N_KV_HEADS * HEAD_DIM
    row = lambda w: pl.BlockSpec((1, ROW_TILE, w), lambda b, j: (b, j, 0))
    tab = pl.BlockSpec((ROW_TILE, HEAD_DIM), lambda b, j: (j, 0))
    shp = lambda w, dt: jax.ShapeDtypeStruct((n_batch, t, w), dt)
    return pl.pallas_call(
        functools.partial(_inproj_kernel, n_batch=n_batch, n_ctx_tiles=n_ctx // ROW_TILE, d=d),
        out_shape=(shp(qw, BF16), shp(kw, BF16), shp(kw, BF16), shp(d, F32),
                   shp(d, BF16), shp(d, BF16), shp(d, BF16)),
        grid=(n_batch, t // ROW_TILE),
        in_specs=[row(d), _const_spec(mod_l.shape), _const_spec(w_in.shape), tab, tab],
        out_specs=(row(qw), row(kw), row(kw), row(d), row(d), row(d), row(d)),
        compiler_params=_params(("parallel", "parallel")),
        name="in_projection",
    )(xa, mod_l, w_in, cos_t, sin_t)


def _attn_kernel(sink_ref, q_ref, kp_ref, kc_ref, kn_ref, kx_ref, vp_ref, vc_ref, vn_ref, vx_ref,
                 o_ref, *, n_ctx_blocks, n_lat_blocks):
    i = pl.program_id(1) - n_ctx_blocks
    n_ctx = kx_ref.shape[1]
    n_win = 3 * ATT_BLOCK
    n_keys = n_win + n_ctx
    rows = KV_GROUP * ATT_BLOCK
    rr = lax.broadcasted_iota(jnp.int32, (rows, n_keys), 0) & (ATT_BLOCK - 1)
    jj = lax.broadcasted_iota(jnp.int32, (rows, n_keys), 1)
    in_band = (jj >= rr) & (jj <= rr + 2 * ATT_BLOCK)
    left_ok = (i > 0) | (jj >= ATT_BLOCK)
    right_ok = (i < n_lat_blocks - 1) | (jj < 2 * ATT_BLOCK)
    allowed = (jj >= n_win) | (in_band & left_ok & right_ok & (i >= 0))
    rblk = lax.broadcasted_iota(jnp.int32, (rows, 1), 0) // ATT_BLOCK
    scale = HEAD_DIM ** -0.5
    q = q_ref[0]
    for h in range(N_KV_HEADS):
        hs = slice(h * HEAD_DIM, (h + 1) * HEAD_DIM)
        kk = jnp.concatenate([kp_ref[0, :, hs], kc_ref[0, :, hs], kn_ref[0, :, hs], kx_ref[0, :, hs]], axis=0)
        vv = jnp.concatenate([vp_ref[0, :, hs], vc_ref[0, :, hs], vn_ref[0, :, hs], vx_ref[0, :, hs]], axis=0)
        heads = [h * KV_GROUP + g for g in range(KV_GROUP)]
        qs = jnp.concatenate([q[:, hd * HEAD_DIM:(hd + 1) * HEAD_DIM] for hd in heads], axis=0)
        s = lax.dot_general(qs, kk, (((1,), (1,)), ((), ())), preferred_element_type=F32) * scale
        s = jnp.where(allowed, s, NEG_BIG)
        sink = jnp.zeros((rows, 1), F32)
        for g, hd in enumerate(heads):
            sink = jnp.where(rblk == g, sink_ref[hd], sink)
        mx = jnp.maximum(jnp.max(s, axis=1, keepdims=True), sink)
        p = jnp.exp(s - mx)
        den = jnp.sum(p, axis=1, keepdims=True) + jnp.exp(sink - mx)
        o = jnp.dot(p.astype(BF16), vv, preferred_element_type=F32) / den
        for g, hd in enumerate(heads):
            o_ref[0, :, hd * HEAD_DIM:(hd + 1) * HEAD_DIM] = o[g * ATT_BLOCK:(g + 1) * ATT_BLOCK].astype(BF16)


def _attention(sink_l, q, k, v, n_ctx):
    n_batch, t, qw = q.shape
    kw = k.shape[2]
    ncb = n_ctx // ATT_BLOCK
    nlb = (t - n_ctx) // ATT_BLOCK

    def lat_blk(off):
        def index(b, j, sink):
            return (b, ncb + jnp.clip(j - ncb + off, 0, nlb - 1), 0)
        return pl.BlockSpec((1, ATT_BLOCK, kw), index)

    ctx_blk = pl.BlockSpec((1, n_ctx, kw), lambda b, j, sink: (b, 0, 0))
    q_blk = pl.BlockSpec((1, ATT_BLOCK, qw), lambda b, j, sink: (b, j, 0))
    kv_specs = [lat_blk(-1), lat_blk(0), lat_blk(1), ctx_blk]
    return pl.pallas_call(
        functools.partial(_attn_kernel, n_ctx_blocks=ncb, n_lat_blocks=nlb),
        out_shape=jax.ShapeDtypeStruct((n_batch, t, qw), BF16),
        grid_spec=pltpu.PrefetchScalarGridSpec(
            num_scalar_prefetch=1,
            grid=(n_batch, t // ATT_BLOCK),
            in_specs=[q_blk] + kv_specs + kv_specs,
            out_specs=q_blk,
        ),
        compiler_params=_params(("parallel", "parallel")),
        name="attention",
    )(sink_l, q, k, k, k, k, v, v, v, v)


def _rnn_kernel(x_ref, cw_ref, cb_ref, wg_ref, ba_ref, bi_ref, lam_ref, hf_ref, hb_ref,
                a_scr, b_scr, h_scr, *, n_ctx, n_rows):
    j = pl.program_id(1)
    tb = ROW_TILE
    n_blk = n_rows // tb
    n_cblk = n_ctx // tb
    blk_f = j
    blk_b = jnp.where(j < n_cblk, n_cblk - 1 - j, n_blk - 1 - (j - n_cblk))
    d = x_ref.shape[2]
    ridx = lax.broadcasted_iota(jnp.int32, (tb, d), 0)

    def conv_block(blk):
        t0 = pl.multiple_of(blk * tb, tb)
        cur = x_ref[0, pl.ds(t0, tb), :]
        prev8 = x_ref[0, pl.ds(pl.multiple_of(jnp.maximum(t0 - 8, 0), 8), 8), :]
        next8 = x_ref[0, pl.ds(pl.multiple_of(jnp.minimum(t0 + tb, n_rows - 8), 8), 8), :]
        prev_ok = (t0 != 0) & (t0 != n_ctx)
        next_ok = (t0 + tb != n_ctx) & (t0 + tb != n_rows)
        prev8 = jnp.where(prev_ok, prev8, 0.0)
        next8 = jnp.where(next_ok, next8, 0.0)
        p6 = jnp.broadcast_to(prev8[6:7, :], (tb, d))
        p7 = jnp.broadcast_to(prev8[7:8, :], (tb, d))
        n0 = jnp.broadcast_to(next8[0:1, :], (tb, d))
        xm1 = jnp.where(ridx == 0, p7, pltpu.roll(cur, 1, 0))
        xm2 = jnp.where(ridx == 0, p6, jnp.where(ridx == 1, p7, pltpu.roll(cur, 2, 0)))
        xp1 = jnp.where(ridx == tb - 1, n0, pltpu.roll(cur, tb - 1, 0))
        return (cb_ref[...] + xm2 * cw_ref[0:1, :] + xm1 * cw_ref[1:2, :]
                + cur * cw_ref[2:3, :] + xp1 * cw_ref[3:4, :])

    def gate_block(xc, dr):
        xb = xc.astype(BF16)
        r_parts, i_parts = [], []
        for jt in range(d // MXU_TILE):
            g = jnp.dot(xb[:, jt * MXU_TILE:(jt + 1) * MXU_TILE], wg_ref[dr, jt],
                        preferred_element_type=F32)
            r_parts.append(g[:, :MXU_TILE])
            i_parts.append(g[:, MXU_TILE:])
        r = jax.nn.sigmoid(jnp.concatenate(r_parts, axis=1) + ba_ref[dr:dr + 1, :])
        ig = jax.nn.sigmoid(jnp.concatenate(i_parts, axis=1) + bi_ref[dr:dr + 1, :])
        nl = -lam_ref[dr:dr + 1, :]
        softplus = jnp.maximum(nl, 0.0) + jnp.log1p(jnp.exp(-jnp.abs(nl)))
        log_a = (-LRU_C) * r * softplus
        a = jnp.exp(log_a)
        a_scr[dr] = a
        b_scr[dr] = jnp.sqrt(1.0 - a * a) * (ig * xc)

    gate_block(conv_block(blk_f), 0)
    gate_block(conv_block(blk_b), 1)

    @pl.when(j == 0)
    def _():
        h_scr[...] = jnp.zeros_like(h_scr)

    def step(s, carry):
        hf, hb = carry
        tf = s
        tr = tb - 1 - s
        hf = a_scr[0, pl.ds(tf, 1), :] * hf + b_scr[0, pl.ds(tf, 1), :]
        hb = a_scr[1, pl.ds(tr, 1), :] * hb + b_scr[1, pl.ds(tr, 1), :]
        hf_ref[0, pl.ds(tf, 1), :] = hf
        hb_ref[0, pl.ds(tr, 1), :] = hb
        return hf, hb

    hf, hb = lax.fori_loop(0, tb, step, (h_scr[0:1, :], h_scr[1:2, :]), unroll=8)
    h_scr[0:1, :] = hf
    h_scr[1:2, :] = hb


def _rnn_branch(xr, conv_w, conv_b, wg, ba, bi, lam, n_ctx):
    n_batch, t, d = xr.shape
    n_blk = t // ROW_TILE
    n_cblk = n_ctx // ROW_TILE

    def bwd_index(b, j):
        return (b, jnp.where(j < n_cblk, n_cblk - 1 - j, n_blk - 1 - (j - n_cblk)), 0)

    out = jax.ShapeDtypeStruct((n_batch, t, d), F32)
    return pl.pallas_call(
        functools.partial(_rnn_kernel, n_ctx=n_ctx, n_rows=t),
        out_shape=(out, out),
        grid=(n_batch, n_blk),
        in_specs=[
            pl.BlockSpec((1, t, d), lambda b, j: (b, 0, 0)),
            _const_spec(conv_w.shape), _const_spec(conv_b.shape), _const_spec(wg.shape),
            _const_spec(ba.shape), _const_spec(bi.shape), _const_spec(lam.shape),
        ],
        out_specs=(pl.BlockSpec((1, ROW_TILE, d), lambda b, j: (b, j, 0)),
                   pl.BlockSpec((1, ROW_TILE, d), bwd_index)),
        scratch_shapes=[pltpu.VMEM((2, ROW_TILE, d), F32), pltpu.VMEM((2, ROW_TILE, d), F32),
                        pltpu.VMEM((8, d), F32)],
        compiler_params=_params(("parallel", "arbitrary")),
        name="rnn_branch",
    )(xr, conv_w, conv_b, wg, ba, bi, lam)


def _dot_nt(a, b):
    return lax.dot_general(a, b, (((1,), (1,)), ((), ())), preferred_element_type=F32)


def _outproj_kernel(x_ref, attn_ref, hf_ref, hb_ref, gy_ref, sa_ref, sr_ref, mod_ref,
                    woa_ref, wor_ref, wout_ref, g_ref, b_ref, rwh_ref, rwl_ref,
                    x1_ref, u2_ref, lg_ref, *, n_batch, n_ctx_tiles, d, alpha):
    m = _mod_row(mod_ref, n_batch, n_ctx_tiles)
    g1 = m[:, 2 * d:3 * d]
    sh2 = m[:, 3 * d:4 * d]
    sc2 = m[:, 4 * d:5 * d]
    a = jnp.dot(attn_ref[0], woa_ref[...], preferred_element_type=F32)
    rnn = ((hf_ref[0] + hb_ref[0]) * gy_ref[0].astype(F32)).astype(BF16)
    r = jnp.dot(rnn, wor_ref[...], preferred_element_type=F32)
    merged = (sa_ref[0].astype(F32) * a + sr_ref[0].astype(F32) * r).astype(BF16)
    mix = jnp.dot(merged, wout_ref[...], preferred_element_type=F32)
    x1 = _layer_norm(alpha * x_ref[0] + g1 * mix, g_ref[...], b_ref[...])
    x1_ref[0] = x1
    u2 = x1 * (1.0 + sc2) + sh2
    u_hi = u2.astype(BF16)
    u2_ref[0] = u_hi
    u_lo = (u2 - u_hi.astype(F32)).astype(BF16)
    lg_ref[...] = (_dot_nt(rwh_ref[...], u_hi) + _dot_nt(rwh_ref[...], u_lo)
                   + _dot_nt(rwl_ref[...], u_hi))


def _out_projection(xa, attn, hf, hb, gy, sa, sr, mod_l, woa, wor, wout, ln_g, ln_b, rwh, rwl,
                    n_ctx, alpha):
    n_batch, t, d = xa.shape
    tiles = t // ROW_TILE
    row = pl.BlockSpec((1, ROW_TILE, d), lambda b, j: (b, j, 0))
    return pl.pallas_call(
        functools.partial(_outproj_kernel, n_batch=n_batch, n_ctx_tiles=n_ctx // ROW_TILE, d=d,
                          alpha=alpha),
        out_shape=(jax.ShapeDtypeStruct((n_batch, t, d), F32),
                   jax.ShapeDtypeStruct((n_batch, t, d), BF16),
                   jax.ShapeDtypeStruct((N_EXPERTS, n_batch * t), F32)),
        grid=(n_batch, tiles),
        in_specs=[row] * 7 + [_const_spec(a.shape) for a in
                              (mod_l, woa, wor, wout, ln_g, ln_b, rwh, rwl)],
        out_specs=(row, row,
                   pl.BlockSpec((N_EXPERTS, ROW_TILE), lambda b, j: (0, b * tiles + j))),
        compiler_params=_params(("parallel", "parallel")),
        name="out_projection",
    )(xa, attn, hf, hb, gy, sa, sr, mod_l, woa, wor, wout, ln_g, ln_b, rwh, rwl)


def _route_kernel(lg_ref, bias_ref, comb_ref):
    lg = lg_ref[...]
    n_tok = lg.shape[1]
    per_group = N_EXPERTS // N_GROUPS
    scores = jax.nn.sigmoid(lg)
    biased = scores + bias_ref[...]
    b3 = biased.reshape(N_GROUPS, per_group, n_tok)
    sub = lax.broadcasted_iota(jnp.int32, b3.shape, 1)
    m1 = jnp.max(b3, axis=1, keepdims=True)
    i1 = jnp.min(jnp.where(b3 == m1, sub, per_group), axis=1, keepdims=True)
    m2 = jnp.max(jnp.where(sub == i1, -jnp.inf, b3), axis=1, keepdims=True)
    grp = (m1 + m2).reshape(N_GROUPS, n_tok)

    gi = lax.broadcasted_iota(jnp.int32, grp.shape, 0)
    gsel = jnp.zeros(grp.shape, F32)
    for _ in range(TOPK_GROUPS):
        m = jnp.max(grp, axis=0, keepdims=True)
        idx = jnp.min(jnp.where(grp == m, gi, N_GROUPS), axis=0, keepdims=True)
        hit = gi == idx
        gsel = jnp.where(hit, 1.0, gsel)
        grp = jnp.where(hit, -jnp.inf, grp)
    emask = jnp.broadcast_to(gsel.reshape(N_GROUPS, 1, n_tok), b3.shape).reshape(N_EXPERTS, n_tok)
    cand = jnp.where(emask > 0.0, biased, -jnp.inf)

    ei = lax.broadcasted_iota(jnp.int32, cand.shape, 0)
    comb = jnp.zeros(cand.shape, F32)
    for _ in range(TOP_K):
        m = jnp.max(cand, axis=0, keepdims=True)
        idx = jnp.min(jnp.where(cand == m, ei, N_EXPERTS), axis=0, keepdims=True)
        hit = ei == idx
        comb = jnp.where(hit, scores, comb)
        cand = jnp.where(hit, -jnp.inf, cand)
    comb = comb / jnp.sum(comb, axis=0, keepdims=True) * ROUTE_SCALE
    padded = jnp.concatenate([comb, jnp.zeros_like(comb)], axis=0)
    comb_ref[...] = padded.T


def _route(logits_t, bias_col):
    n_tok = logits_t.shape[1]
    tile = 1024
    return pl.pallas_call(
        _route_kernel,
        out_shape=jax.ShapeDtypeStruct((n_tok, 2 * N_EXPERTS), F32),
        grid=(n_tok // tile,),
        in_specs=[pl.BlockSpec((N_EXPERTS, tile), lambda i: (0, i)), _const_spec(bias_col.shape)],
        out_specs=pl.BlockSpec((tile, 2 * N_EXPERTS), lambda i: (i, 0)),
        compiler_params=_params(("parallel",)),
        name="route",
    )(logits_t, bias_col)


def _moe_kernel(x_ref, comb_ref, w13_ref, w2_ref, o_ref, acc_ref):
    e = pl.program_id(1)

    @pl.when(e == 0)
    def _():
        acc_ref[...] = jnp.zeros_like(acc_ref)

    h13 = jnp.dot(x_ref[...], w13_ref[0], preferred_element_type=F32)
    h = (_silu(h13[:, :D_EXPERT]) * h13[:, D_EXPERT:]).astype(BF16)
    y = jnp.dot(h, w2_ref[0], preferred_element_type=F32)
    comb = comb_ref[...]
    lane = lax.broadcasted_iota(jnp.int32, comb.shape, 1)
    gate = jnp.sum(jnp.where(lane == e, comb, 0.0), axis=1, keepdims=True)
    gate = jnp.where(e == N_EXPERTS, 1.0, gate)
    acc_ref[...] += gate * y

    @pl.when(e == pl.num_programs(1) - 1)
    def _():
        o_ref[...] = acc_ref[...]


def _moe(u2, comb, w13, w2):
    n_tok, d = u2.shape
    n_slots = w13.shape[0]
    tile = MOE_ROW_TILE
    return pl.pallas_call(
        _moe_kernel,
        out_shape=jax.ShapeDtypeStruct((n_tok, d), F32),
        grid=(n_tok // tile, n_slots),
        in_specs=[
            pl.BlockSpec((tile, d), lambda i, e: (i, 0)),
            pl.BlockSpec((tile, 2 * N_EXPERTS), lambda i, e: (i, 0)),
            pl.BlockSpec((1, d, 2 * D_EXPERT), lambda i, e: (e, 0, 0)),
            pl.BlockSpec((1, D_EXPERT, d), lambda i, e: (e, 0, 0)),
        ],
        out_specs=pl.BlockSpec((tile, d), lambda i, e: (i, 0)),
        scratch_shapes=[pltpu.VMEM((tile, d), F32)],
        compiler_params=_params(("parallel", "arbitrary")),
        name="experts",
    )(u2, comb, w13, w2)


def _ffn_norm_kernel(x_ref, f_ref, mod_ref, g_ref, b_ref, o_ref, *, n_batch, n_ctx_tiles, d, alpha):
    m = _mod_row(mod_ref, n_batch, n_ctx_tiles)
    g2 = m[:, 5 * d:6 * d]
    o_ref[0] = _layer_norm(alpha * x_ref[0] + g2 * f_ref[0], g_ref[...], b_ref[...])


def _ffn_norm(x1, ffn, mod_l, ln_g, ln_b, n_ctx, alpha):
    n_batch, t, d = x1.shape
    row = pl.BlockSpec((1, ROW_TILE, d), lambda b, j: (b, j, 0))
    return pl.pallas_call(
        functools.partial(_ffn_norm_kernel, n_batch=n_batch, n_ctx_tiles=n_ctx // ROW_TILE, d=d,
                          alpha=alpha),
        out_shape=jax.ShapeDtypeStruct((n_batch, t, d), F32),
        grid=(n_batch, t // ROW_TILE),
        in_specs=[row, row, _const_spec(mod_l.shape), _const_spec(ln_g.shape), _const_spec(ln_b.shape)],
        out_specs=row,
        compiler_params=_params(("parallel", "parallel")),
        name="ffn_norm",
    )(x1, ffn, mod_l, ln_g, ln_b)


def _rope_tables(n_ctx, n_lat):
    pos = jnp.arange(n_lat)
    n_freq = HEAD_DIM // 4
    inv = ROPE_THETA ** (-jnp.arange(n_freq, dtype=F32) / n_freq)
    ang_r = (pos // GRID_W).astype(F32)[:, None] * inv
    ang_c = (pos % GRID_W).astype(F32)[:, None] * inv
    cos_l = jnp.concatenate([jnp.cos(ang_r)] * 2 + [jnp.cos(ang_c)] * 2, axis=1)
    sin_l = jnp.concatenate([-jnp.sin(ang_r), jnp.sin(ang_r), -jnp.sin(ang_c), jnp.sin(ang_c)], axis=1)
    cos_t = jnp.concatenate([jnp.ones((n_ctx, HEAD_DIM), F32), cos_l], axis=0)
    sin_t = jnp.concatenate([jnp.zeros((n_ctx, HEAD_DIM), F32), sin_l], axis=0)
    return cos_t, sin_t


def _pack_gate_weights(wa, wi):
    n_dir, n_blocks, w, _ = wa.shape
    per_tile = MXU_TILE // w
    n_tiles = n_blocks // per_tile

    def dense(wx):
        wx = wx.reshape(n_dir, n_tiles, per_tile, w, w)
        eye = jnp.eye(per_tile, dtype=wx.dtype)
        full = jnp.einsum('dtpij,pq->dtpiqj', wx, eye)
        return full.reshape(n_dir, n_tiles, MXU_TILE, MXU_TILE)

    return jnp.concatenate([dense(wa), dense(wi)], axis=-1).astype(BF16)


def kernel(x, c, ctx, c_ctx, w_mod, b_mod, w_in, sink, conv_w, conv_b, rg_wa, rg_ba, rg_wi, rg_bi,
           rg_lambda, w_o_attn, w_o_rnn, w_out, ln1_g, ln1_b, router_w, router_bias, exp_w1, exp_w3,
           exp_w2, sh_w1, sh_w3, sh_w2, ln2_g, ln2_b):
    n_batch, n_lat, d = x.shape
    n_ctx = ctx.shape[1]
    n_layers = w_mod.shape[0]
    t = n_ctx + n_lat
    assert n_ctx % ROW_TILE == 0 and n_lat % ROW_TILE == 0 and n_batch + 1 <= MOD_ROWS
    assert (n_batch * t) % MOE_ROW_TILE == 0
    alpha = (2 * n_layers) ** 0.25

    xa = jnp.concatenate([ctx, x], axis=1)
    cc = jnp.zeros((MOD_ROWS, d), F32).at[:n_batch].set(c).at[n_batch].set(c_ctx)
    mod = _modulation(cc, w_mod, b_mod)
    cos_t, sin_t = _rope_tables(n_ctx, n_lat)

    for l in range(n_layers):
        mod_l = mod[l]
        q, k, v, xr, gy, sa, sr = _in_projection(xa, mod_l, w_in[l].astype(BF16), cos_t, sin_t, n_ctx)
        attn = _attention(sink[l], q, k, v, n_ctx)
        wg = _pack_gate_weights(rg_wa[l], rg_wi[l])
        hf, hb = _rnn_branch(xr, conv_w[l], conv_b[l][None, :], wg, rg_ba[l], rg_bi[l], rg_lambda[l],
                             n_ctx)
        rw_t = router_w[l].T
        rw_hi = rw_t.astype(BF16)
        rw_lo = (rw_t - rw_hi.astype(F32)).astype(BF16)
        x1, u2, logits_t = _out_projection(
            xa, attn, hf, hb, gy, sa, sr, mod_l, w_o_attn[l].astype(BF16), w_o_rnn[l].astype(BF16),
            w_out[l].astype(BF16), ln1_g[l][None, :], ln1_b[l][None, :], rw_hi, rw_lo, n_ctx, alpha)
        comb = _route(logits_t, router_bias[l][:, None])
        w13 = jnp.concatenate(
            [jnp.concatenate([exp_w1[l], exp_w3[l]], axis=-1),
             jnp.concatenate([sh_w1[l], sh_w3[l]], axis=-1)[None]], axis=0).astype(BF16)
        w2 = jnp.concatenate([exp_w2[l], sh_w2[l][None]], axis=0).astype(BF16)
        ffn = _moe(u2.reshape(n_batch * t, d), comb, w13, w2)
        xa = _ffn_norm(x1, ffn.reshape(n_batch, t, d), mod_l, ln2_g[l][None, :], ln2_b[l][None, :],
                       n_ctx, alpha)
    return xa[:, n_ctx:, :]
```

```python
import functools
import math

import jax
import jax.numpy as jnp
from jax import lax
from jax.experimental import pallas as pl
from jax.experimental.pallas import tpu as pltpu

N_HEADS = 8
N_KV_HEADS = 2
HEAD_DIM = 128
KV_GROUP = N_HEADS // N_KV_HEADS
ATT_BLOCK = 128
GRID_W = 64
ROPE_THETA = 10000.0
RNN_BLOCKS = 16
LRU_C = 8.0
N_EXPERTS = 64
TOP_K = 8
N_GROUPS = 8
TOPK_GROUPS = 4
D_EXPERT = 256
ROUTE_SCALE = 2.5
LN_EPS = 1e-6

ROW_TILE = 256
MXU_TILE = 256
CHUNK = 16
LOCAL_ROWS = TOP_K * ROW_TILE + N_EXPERTS * CHUNK
EXPERT_ROW_TILE = 512
MOD_ROWS = 24
VMEM_LIMIT = 56 * 1024 * 1024

F32 = jnp.float32
BF16 = jnp.bfloat16
NEG_BIG = -1e30


def _const_spec(shape):
    zeros = (0,) * len(shape)
    return pl.BlockSpec(shape, lambda *_: zeros, pipeline_mode=pl.Buffered(1))


def _params(sem):
    return pltpu.CompilerParams(dimension_semantics=sem, vmem_limit_bytes=VMEM_LIMIT)


def _silu(v):
    return v * jax.nn.sigmoid(v)


def _gelu_tanh(v):
    return v * (0.5 * (1.0 + jnp.tanh(math.sqrt(2.0 / math.pi) * (v + 0.044715 * (v * v * v)))))


def _layer_norm(v, g, b):
    mu = jnp.mean(v, axis=-1, keepdims=True)
    d = v - mu
    var = jnp.mean(d * d, axis=-1, keepdims=True)
    return d * lax.rsqrt(var + LN_EPS) * g + b


def _mod_kernel(cc_ref, w_ref, b_ref, o_ref):
    s = _silu(cc_ref[...]).astype(BF16)
    o_ref[0] = jnp.dot(s, w_ref[0].astype(BF16), preferred_element_type=F32) + b_ref[0]


def _modulation(cc, w_mod, b_mod):
    n_layers, d, d6 = w_mod.shape
    col = 1536
    return pl.pallas_call(
        _mod_kernel,
        out_shape=jax.ShapeDtypeStruct((n_layers, MOD_ROWS, d6), F32),
        grid=(n_layers, d6 // col),
        in_specs=[
            pl.BlockSpec((MOD_ROWS, d), lambda l, j: (0, 0)),
            pl.BlockSpec((1, d, col), lambda l, j: (l, 0, j)),
            pl.BlockSpec((1, 1, col), lambda l, j: (l, 0, j)),
        ],
        out_specs=pl.BlockSpec((1, MOD_ROWS, col), lambda l, j: (l, 0, j)),
        compiler_params=_params(("parallel", "parallel")),
        name="modulation",
    )(cc, w_mod, b_mod.reshape(n_layers, 1, d6))


def _mod_row(mod_ref, n_batch, n_ctx_tiles):
    b = pl.program_id(0)
    j = pl.program_id(1)
    row = jnp.where(j < n_ctx_tiles, n_batch, b)
    return mod_ref[pl.ds(row, 1), :]


def _rope(v, cos_w, sin_w):
    width = v.shape[1]
    lane = lax.broadcasted_iota(jnp.int32, v.shape, 1)
    partner = jnp.where((lane & 32) == 0, pltpu.roll(v, width - 32, 1), pltpu.roll(v, 32, 1))
    return v * cos_w + partner * sin_w


def _inproj_kernel(x_ref, mod_ref, w_ref, cos_ref, sin_ref,
                   q_ref, k_ref, v_ref, xr_ref, gy_ref, sa_ref, sr_ref,
                   *, n_batch, n_ctx_tiles, d):
    m = _mod_row(mod_ref, n_batch, n_ctx_tiles)
    sh1 = m[:, 0:d]
    sc1 = m[:, d:2 * d]
    u = (x_ref[0] * (1.0 + sc1) + sh1).astype(BF16)

    def proj(lo, hi):
        return jnp.dot(u, w_ref[:, lo:hi], preferred_element_type=F32)

    qw = N_HEADS * HEAD_DIM
    kw = N_KV_HEADS * HEAD_DIM
    cos1 = cos_ref[...]
    sin1 = sin_ref[...]
    o = 0
    q = proj(o, o + qw)
    q_ref[0] = _rope(q, jnp.concatenate([cos1] * N_HEADS, axis=1),
                     jnp.concatenate([sin1] * N_HEADS, axis=1)).astype(BF16)
    o += qw
    k = proj(o, o + kw)
    k_ref[0] = _rope(k, jnp.concatenate([cos1] * N_KV_HEADS, axis=1),
                     jnp.concatenate([sin1] * N_KV_HEADS, axis=1)).astype(BF16)
    o += kw
    v_ref[0] = proj(o, o + kw).astype(BF16)
    o += kw
    xr_ref[0] = proj(o, o + d)
    o += d
    gy_ref[0] = _gelu_tanh(proj(o, o + d)).astype(BF16)
    o += d
    sa_ref[0] = jax.nn.sigmoid(proj(o, o + d)).astype(BF16)
    o += d
    sr_ref[0] = jax.nn.sigmoid(proj(o, o + d)).astype(BF16)


def _in_projection(xa, mod_l, w_in, cos_t, sin_t, n_ctx):
    n_batch, t, d = xa.shape
    qw = N_HEADS * HEAD_DIM
    kw = N_KV_HEADS * HEAD_DIM
    row = lambda w: pl.BlockSpec((1, ROW_TILE, w), lambda b, j: (b, j, 0))
    tab = pl.BlockSpec((ROW_TILE, HEAD_DIM), lambda b, j: (j, 0))
    shp = lambda w, dt: jax.ShapeDtypeStruct((n_batch, t, w), dt)
    return pl.pallas_call(
        functools.partial(_inproj_kernel, n_batch=n_batch, n_ctx_tiles=n_ctx // ROW_TILE, d=d),
        out_shape=(shp(qw, BF16), shp(kw, BF16), shp(kw, BF16), shp(d, F32),
                   shp(d, BF16), shp(d, BF16), shp(d, BF16)),
        grid=(n_batch, t // ROW_TILE),
        in_specs=[row(d), _const_spec(mod_l.shape), _const_spec(w_in.shape), tab, tab],
        out_specs=(row(qw), row(kw), row(kw), row(d), row(d), row(d), row(d)),
        compiler_params=_params(("parallel", "parallel")),
        name="in_projection",
    )(xa, mod_l, w_in, cos_t, sin_t)


def _attn_kernel(sink_ref, q_ref, kp_ref, kc_ref, kn_ref, kx_ref, vp_ref, vc_ref, vn_ref, vx_ref,
                 o_ref, *, n_ctx_blocks, n_lat_blocks):
    i = pl.program_id(1) - n_ctx_blocks
    n_ctx = kx_ref.shape[1]
    n_win = 3 * ATT_BLOCK
    n_keys = n_win + n_ctx
    rows = KV_GROUP * ATT_BLOCK
    rr = lax.broadcasted_iota(jnp.int32, (rows, n_keys), 0) & (ATT_BLOCK - 1)
    jj = lax.broadcasted_iota(jnp.int32, (rows, n_keys), 1)
    in_band = (jj >= rr) & (jj <= rr + 2 * ATT_BLOCK)
    left_ok = (i > 0) | (jj >= ATT_BLOCK)
    right_ok = (i < n_lat_blocks - 1) | (jj < 2 * ATT_BLOCK)
    allowed = (jj >= n_win) | (in_band & left_ok & right_ok & (i >= 0))
    rblk = lax.broadcasted_iota(jnp.int32, (rows, 1), 0) // ATT_BLOCK
    scale = HEAD_DIM ** -0.5
    q = q_ref[0]
    for h in range(N_KV_HEADS):
        hs = slice(h * HEAD_DIM, (h + 1) * HEAD_DIM)
        kk = jnp.concatenate([kp_ref[0, :, hs], kc_ref[0, :, hs], kn_ref[0, :, hs], kx_ref[0, :, hs]], axis=0)
        vv = jnp.concatenate([vp_ref[0, :, hs], vc_ref[0, :, hs], vn_ref[0, :, hs], vx_ref[0, :, hs]], axis=0)
        heads = [h * KV_GROUP + g for g in range(KV_GROUP)]
        qs = jnp.concatenate([q[:, hd * HEAD_DIM:(hd + 1) * HEAD_DIM] for hd in heads], axis=0)
        s = lax.dot_general(qs, kk, (((1,), (1,)), ((), ())), preferred_element_type=F32) * scale
        s = jnp.where(allowed, s, NEG_BIG)
        sink = jnp.zeros((rows, 1), F32)
        for g, hd in enumerate(heads):
            sink = jnp.where(rblk == g, sink_ref[hd], sink)
        mx = jnp.maximum(jnp.max(s, axis=1, keepdims=True), sink)
        p = jnp.exp(s - mx)
        den = jnp.sum(p, axis=1, keepdims=True) + jnp.exp(sink - mx)
        o = jnp.dot(p.astype(BF16), vv, preferred_element_type=F32) / den
        for g, hd in enumerate(heads):
            o_ref[0, :, hd * HEAD_DIM:(hd + 1) * HEAD_DIM] = o[g * ATT_BLOCK:(g + 1) * ATT_BLOCK].astype(BF16)


def _attention(sink_l, q, k, v, n_ctx):
    n_batch, t, qw = q.shape
    kw = k.shape[2]
    ncb = n_ctx // ATT_BLOCK
    nlb = (t - n_ctx) // ATT_BLOCK

    def lat_blk(off):
        def index(b, j, sink):
            return (b, ncb + jnp.clip(j - ncb + off, 0, nlb - 1), 0)
        return pl.BlockSpec((1, ATT_BLOCK, kw), index)

    ctx_blk = pl.BlockSpec((1, n_ctx, kw), lambda b, j, sink: (b, 0, 0))
    q_blk = pl.BlockSpec((1, ATT_BLOCK, qw), lambda b, j, sink: (b, j, 0))
    kv_specs = [lat_blk(-1), lat_blk(0), lat_blk(1), ctx_blk]
    return pl.pallas_call(
        functools.partial(_attn_kernel, n_ctx_blocks=ncb, n_lat_blocks=nlb),
        out_shape=jax.ShapeDtypeStruct((n_batch, t, qw), BF16),
        grid_spec=pltpu.PrefetchScalarGridSpec(
            num_scalar_prefetch=1,
            grid=(n_batch, t // ATT_BLOCK),
            in_specs=[q_blk] + kv_specs + kv_specs,
            out_specs=q_blk,
        ),
        compiler_params=_params(("parallel", "parallel")),
        name="attention",
    )(sink_l, q, k, k, k, k, v, v, v, v)


def _rnn_kernel(x_ref, cw_ref, cb_ref, wg_ref, ba_ref, bi_ref, lam_ref, hf_ref, hb_ref,
                a_scr, b_scr, h_scr, *, n_ctx, n_rows):
    j = pl.program_id(1)
    tb = ROW_TILE
    n_blk = n_rows // tb
    n_cblk = n_ctx // tb
    blk_f = j
    blk_b = jnp.where(j < n_cblk, n_cblk - 1 - j, n_blk - 1 - (j - n_cblk))
    d = x_ref.shape[2]
    ridx = lax.broadcasted_iota(jnp.int32, (tb, d), 0)

    def conv_block(blk):
        t0 = pl.multiple_of(blk * tb, tb)
        cur = x_ref[0, pl.ds(t0, tb), :]
        prev8 = x_ref[0, pl.ds(pl.multiple_of(jnp.maximum(t0 - 8, 0), 8), 8), :]
        next8 = x_ref[0, pl.ds(pl.multiple_of(jnp.minimum(t0 + tb, n_rows - 8), 8), 8), :]
        prev_ok = (t0 != 0) & (t0 != n_ctx)
        next_ok = (t0 + tb != n_ctx) & (t0 + tb != n_rows)
        prev8 = jnp.where(prev_ok, prev8, 0.0)
        next8 = jnp.where(next_ok, next8, 0.0)
        p6 = jnp.broadcast_to(prev8[6:7, :], (tb, d))
        p7 = jnp.broadcast_to(prev8[7:8, :], (tb, d))
        n0 = jnp.broadcast_to(next8[0:1, :], (tb, d))
        xm1 = jnp.where(ridx == 0, p7, pltpu.roll(cur, 1, 0))
        xm2 = jnp.where(ridx == 0, p6, jnp.where(ridx == 1, p7, pltpu.roll(cur, 2, 0)))
        xp1 = jnp.where(ridx == tb - 1, n0, pltpu.roll(cur, tb - 1, 0))
        return (cb_ref[...] + xm2 * cw_ref[0:1, :] + xm1 * cw_ref[1:2, :]
                + cur * cw_ref[2:3, :] + xp1 * cw_ref[3:4, :])

    def gate_block(xc, dr):
        xb = xc.astype(BF16)
        r_parts, i_parts = [], []
        for jt in range(d // MXU_TILE):
            g = jnp.dot(xb[:, jt * MXU_TILE:(jt + 1) * MXU_TILE], wg_ref[dr, jt],
                        preferred_element_type=F32)
            r_parts.append(g[:, :MXU_TILE])
            i_parts.append(g[:, MXU_TILE:])
        r = jax.nn.sigmoid(jnp.concatenate(r_parts, axis=1) + ba_ref[dr:dr + 1, :])
        ig = jax.nn.sigmoid(jnp.concatenate(i_parts, axis=1) + bi_ref[dr:dr + 1, :])
        nl = -lam_ref[dr:dr + 1, :]
        softplus = jnp.maximum(nl, 0.0) + jnp.log1p(jnp.exp(-jnp.abs(nl)))
        log_a = (-LRU_C) * r * softplus
        a = jnp.exp(log_a)
        a_scr[dr] = a
        b_scr[dr] = jnp.sqrt(1.0 - a * a) * (ig * xc)

    gate_block(conv_block(blk_f), 0)
    gate_block(conv_block(blk_b), 1)

    @pl.when(j == 0)
    def _():
        h_scr[...] = jnp.zeros_like(h_scr)

    def step(s, carry):
        hf, hb = carry
        tf = s
        tr = tb - 1 - s
        hf = a_scr[0, pl.ds(tf, 1), :] * hf + b_scr[0, pl.ds(tf, 1), :]
        hb = a_scr[1, pl.ds(tr, 1), :] * hb + b_scr[1, pl.ds(tr, 1), :]
        hf_ref[0, pl.ds(tf, 1), :] = hf
        hb_ref[0, pl.ds(tr, 1), :] = hb
        return hf, hb

    hf, hb = lax.fori_loop(0, tb, step, (h_scr[0:1, :], h_scr[1:2, :]), unroll=8)
    h_scr[0:1, :] = hf
    h_scr[1:2, :] = hb


def _rnn_branch(xr, conv_w, conv_b, wg, ba, bi, lam, n_ctx):
    n_batch, t, d = xr.shape
    n_blk = t // ROW_TILE
    n_cblk = n_ctx // ROW_TILE

    def bwd_index(b, j):
        return (b, jnp.where(j < n_cblk, n_cblk - 1 - j, n_blk - 1 - (j - n_cblk)), 0)

    out = jax.ShapeDtypeStruct((n_batch, t, d), F32)
    return pl.pallas_call(
        functools.partial(_rnn_kernel, n_ctx=n_ctx, n_rows=t),
        out_shape=(out, out),
        grid=(n_batch, n_blk),
        in_specs=[
            pl.BlockSpec((1, t, d), lambda b, j: (b, 0, 0)),
            _const_spec(conv_w.shape), _const_spec(conv_b.shape), _const_spec(wg.shape),
            _const_spec(ba.shape), _const_spec(bi.shape), _const_spec(lam.shape),
        ],
        out_specs=(pl.BlockSpec((1, ROW_TILE, d), lambda b, j: (b, j, 0)),
                   pl.BlockSpec((1, ROW_TILE, d), bwd_index)),
        scratch_shapes=[pltpu.VMEM((2, ROW_TILE, d), F32), pltpu.VMEM((2, ROW_TILE, d), F32),
                        pltpu.VMEM((8, d), F32)],
        compiler_params=_params(("parallel", "arbitrary")),
        name="rnn_branch",
    )(xr, conv_w, conv_b, wg, ba, bi, lam)


def _dot_nt(a, b):
    return lax.dot_general(a, b, (((1,), (1,)), ((), ())), preferred_element_type=F32)


def _outproj_kernel(x_ref, attn_ref, hf_ref, hb_ref, gy_ref, sa_ref, sr_ref, mod_ref,
                    woa_ref, wor_ref, wout_ref, g_ref, b_ref, rwh_ref, rwl_ref,
                    x1_ref, u2_ref, lg_ref, *, n_batch, n_ctx_tiles, d, alpha):
    m = _mod_row(mod_ref, n_batch, n_ctx_tiles)
    g1 = m[:, 2 * d:3 * d]
    sh2 = m[:, 3 * d:4 * d]
    sc2 = m[:, 4 * d:5 * d]
    a = jnp.dot(attn_ref[0], woa_ref[...], preferred_element_type=F32)
    rnn = ((hf_ref[0] + hb_ref[0]) * gy_ref[0].astype(F32)).astype(BF16)
    r = jnp.dot(rnn, wor_ref[...], preferred_element_type=F32)
    merged = (sa_ref[0].astype(F32) * a + sr_ref[0].astype(F32) * r).astype(BF16)
    mix = jnp.dot(merged, wout_ref[...], preferred_element_type=F32)
    x1 = _layer_norm(alpha * x_ref[0] + g1 * mix, g_ref[...], b_ref[...])
    x1_ref[0] = x1
    u2 = x1 * (1.0 + sc2) + sh2
    u_hi = u2.astype(BF16)
    u2_ref[0] = u_hi
    u_lo = (u2 - u_hi.astype(F32)).astype(BF16)
    lg_ref[...] = (_dot_nt(rwh_ref[...], u_hi) + _dot_nt(rwh_ref[...], u_lo)
                   + _dot_nt(rwl_ref[...], u_hi))


def _out_projection(xa, attn, hf, hb, gy, sa, sr, mod_l, woa, wor, wout, ln_g, ln_b, rwh, rwl,
                    n_ctx, alpha):
    n_batch, t, d = xa.shape
    tiles = t // ROW_TILE
    row = pl.BlockSpec((1, ROW_TILE, d), lambda b, j: (b, j, 0))
    return pl.pallas_call(
        functools.partial(_outproj_kernel, n_batch=n_batch, n_ctx_tiles=n_ctx // ROW_TILE, d=d,
                          alpha=alpha),
        out_shape=(jax.ShapeDtypeStruct((n_batch, t, d), F32),
                   jax.ShapeDtypeStruct((n_batch, t, d), BF16),
                   jax.ShapeDtypeStruct((N_EXPERTS, n_batch * t), F32)),
        grid=(n_batch, tiles),
        in_specs=[row] * 7 + [_const_spec(a.shape) for a in
                              (mod_l, woa, wor, wout, ln_g, ln_b, rwh, rwl)],
        out_specs=(row, row,
                   pl.BlockSpec((N_EXPERTS, ROW_TILE), lambda b, j: (0, b * tiles + j))),
        compiler_params=_params(("parallel", "parallel")),
        name="out_projection",
    )(xa, attn, hf, hb, gy, sa, sr, mod_l, woa, wor, wout, ln_g, ln_b, rwh, rwl)


def _route_kernel(lg_ref, bias_ref, pos_ref, pwt_ref, nch_ref, lo_ref):
    lg = lg_ref[...]
    n_tok = lg.shape[1]
    per_group = N_EXPERTS // N_GROUPS
    scores = jax.nn.sigmoid(lg)
    biased = scores + bias_ref[...]
    b3 = biased.reshape(N_GROUPS, per_group, n_tok)
    sub = lax.broadcasted_iota(jnp.int32, b3.shape, 1)
    m1 = jnp.max(b3, axis=1, keepdims=True)
    i1 = jnp.min(jnp.where(b3 == m1, sub, per_group), axis=1, keepdims=True)
    m2 = jnp.max(jnp.where(sub == i1, -jnp.inf, b3), axis=1, keepdims=True)
    grp = (m1 + m2).reshape(N_GROUPS, n_tok)

    gi = lax.broadcasted_iota(jnp.int32, grp.shape, 0)
    gsel = jnp.zeros(grp.shape, F32)
    for _ in range(TOPK_GROUPS):
        m = jnp.max(grp, axis=0, keepdims=True)
        idx = jnp.min(jnp.where(grp == m, gi, N_GROUPS), axis=0, keepdims=True)
        hit = gi == idx
        gsel = jnp.where(hit, 1.0, gsel)
        grp = jnp.where(hit, -jnp.inf, grp)
    emask = jnp.broadcast_to(gsel.reshape(N_GROUPS, 1, n_tok), b3.shape).reshape(N_EXPERTS, n_tok)
    cand = jnp.where(emask > 0.0, biased, -jnp.inf)

    ei = lax.broadcasted_iota(jnp.int32, cand.shape, 0)
    comb = jnp.zeros(cand.shape, F32)
    picked = jnp.zeros(cand.shape, F32)
    hits = []
    for _ in range(TOP_K):
        m = jnp.max(cand, axis=0, keepdims=True)
        idx = jnp.min(jnp.where(cand == m, ei, N_EXPERTS), axis=0, keepdims=True)
        hit = ei == idx
        hits.append(hit)
        comb = jnp.where(hit, scores, comb)
        picked = jnp.where(hit, 1.0, picked)
        cand = jnp.where(hit, -jnp.inf, cand)
    comb = comb / jnp.sum(comb, axis=0, keepdims=True) * ROUTE_SCALE

    cnt = jnp.sum(picked, axis=1, keepdims=True)
    nch = jnp.broadcast_to(jnp.floor((cnt + (CHUNK - 1)) * (1.0 / CHUNK)), (N_EXPERTS, 128))
    erow = lax.broadcasted_iota(jnp.int32, nch.shape, 0)
    incl = nch
    for s in (1, 2, 4, 8, 16, 32):
        incl = incl + jnp.where(erow >= s, pltpu.roll(incl, s, 0), 0.0)
    lo = incl - nch
    earlier = (lax.broadcasted_iota(jnp.int32, (n_tok, n_tok), 0)
               < lax.broadcasted_iota(jnp.int32, (n_tok, n_tok), 1)).astype(BF16)
    rank = jnp.dot(picked.astype(BF16), earlier, preferred_element_type=F32)
    slot = lo[:, 0:1] * CHUNK + rank
    pos = jnp.concatenate([jnp.sum(jnp.where(h, slot, 0.0), axis=0, keepdims=True) for h in hits], axis=0)
    wts = jnp.concatenate([jnp.sum(jnp.where(h, comb, 0.0), axis=0, keepdims=True) for h in hits], axis=0)
    pos_ref[...] = pos.astype(jnp.int32)
    pad = jnp.zeros((128 - 2 * TOP_K, n_tok), F32)
    pwt_ref[...] = jnp.concatenate([pos, wts, pad], axis=0).T
    nch_ref[0] = nch.astype(jnp.int32)
    lo_ref[0] = lo.astype(jnp.int32)


def _route(logits_t, bias_col):
    n_tok = logits_t.shape[1]
    n_tiles = n_tok // ROW_TILE
    tab = jax.ShapeDtypeStruct((n_tiles, N_EXPERTS, 128), jnp.int32)
    tab_spec = pl.BlockSpec((1, N_EXPERTS, 128), lambda i: (i, 0, 0))
    return pl.pallas_call(
        _route_kernel,
        out_shape=(jax.ShapeDtypeStruct((TOP_K, n_tok), jnp.int32),
                   jax.ShapeDtypeStruct((n_tok, 128), F32), tab, tab),
        grid=(n_tiles,),
        in_specs=[pl.BlockSpec((N_EXPERTS, ROW_TILE), lambda i: (0, i)), _const_spec(bias_col.shape)],
        out_specs=(pl.BlockSpec((TOP_K, ROW_TILE), lambda i: (0, i)),
                   pl.BlockSpec((ROW_TILE, 128), lambda i: (i, 0)), tab_spec, tab_spec),
        compiler_params=_params(("parallel",)),
        name="route",
    )(logits_t, bias_col)


def _chunk_copy(src_ref, src_chunk, dst_ref, dst_chunk, sem):
    return pltpu.make_async_copy(
        src_ref.at[pl.ds(pl.multiple_of(src_chunk * CHUNK, CHUNK), CHUNK), :],
        dst_ref.at[pl.ds(pl.multiple_of(dst_chunk * CHUNK, CHUNK), CHUNK), :], sem)


def _move_tile_chunks(i, lo_tab, nch_tab, g_tab, local_ref, global_ref, sem, to_global):
    def per_expert(e, carry):
        n = nch_tab[i * N_EXPERTS + e]
        lo = lo_tab[i * N_EXPERTS + e]
        g = g_tab[i * N_EXPERTS + e]

        def per_chunk(c, carry2):
            if to_global:
                _chunk_copy(local_ref, lo + c, global_ref, g + c, sem).start()
            else:
                _chunk_copy(global_ref, g + c, local_ref, lo + c, sem).start()
            return carry2

        return lax.fori_loop(0, n, per_chunk, carry)

    lax.fori_loop(0, N_EXPERTS, per_expert, 0)


def _wait_chunks(n, src_ref, dst_ref, sem):
    def body(c, carry):
        _chunk_copy(src_ref, 0, dst_ref, 0, sem).wait()
        return carry

    lax.fori_loop(0, n, body, 0)


def _flat_tile():
    return pl.program_id(0) * pl.num_programs(1) + pl.program_id(1)


def _dispatch_kernel(lo_tab, nch_tab, g_tab, tot_tab, tail_lo, tail_n,
                     x_ref, pos_ref, xs_hbm, loc, zbuf, sem):
    i = _flat_tile()
    n_tiles = pl.num_programs(0) * pl.num_programs(1)
    tot = tot_tab[i]
    n_blk = (tot * CHUNK + ROW_TILE - 1) // ROW_TILE
    pos = pos_ref[...]
    x = x_ref[0]
    rid0 = lax.broadcasted_iota(jnp.int32, (ROW_TILE, ROW_TILE), 0)

    def build(rb, carry):
        r0 = pl.multiple_of(rb * ROW_TILE, ROW_TILE)
        rid = rid0 + r0
        p = jnp.zeros((ROW_TILE, ROW_TILE), F32)
        for k in range(TOP_K):
            p = jnp.where(rid == pos[k:k + 1, :], 1.0, p)
        loc[pl.ds(r0, ROW_TILE), :] = jnp.dot(p.astype(BF16), x, preferred_element_type=F32).astype(BF16)
        return carry

    lax.fori_loop(0, n_blk, build, 0)
    _move_tile_chunks(i, lo_tab, nch_tab, g_tab, loc, xs_hbm, sem, to_global=True)
    _wait_chunks(tot, loc, xs_hbm, sem)

    @pl.when(i == n_tiles - 1)
    def _():
        zbuf[...] = jnp.zeros_like(zbuf)

        def per_expert(e, n_started):
            def per_chunk(c, carry):
                _chunk_copy(zbuf, 0, xs_hbm, tail_lo[e] + c, sem).start()
                return carry

            lax.fori_loop(0, tail_n[e], per_chunk, 0)
            return n_started + tail_n[e]

        n_started = lax.fori_loop(0, N_EXPERTS, per_expert, 0)
        _wait_chunks(n_started, zbuf, xs_hbm, sem)


def _dispatch(u2, pos, tabs, n_row_tiles):
    n_batch, t, d = u2.shape
    tiles = t // ROW_TILE
    return pl.pallas_call(
        _dispatch_kernel,
        out_shape=jax.ShapeDtypeStruct((n_row_tiles * EXPERT_ROW_TILE, d), BF16),
        grid_spec=pltpu.PrefetchScalarGridSpec(
            num_scalar_prefetch=6,
            grid=(n_batch, tiles),
            in_specs=[pl.BlockSpec((1, ROW_TILE, d), lambda b, j, *_: (b, j, 0)),
                      pl.BlockSpec((TOP_K, ROW_TILE), lambda b, j, *_: (0, b * tiles + j))],
            out_specs=pl.BlockSpec(memory_space=pl.ANY),
            scratch_shapes=[pltpu.VMEM((LOCAL_ROWS, d), BF16), pltpu.VMEM((CHUNK, d), BF16),
                            pltpu.SemaphoreType.DMA],
        ),
        compiler_params=_params(("arbitrary", "arbitrary")),
        name="dispatch",
    )(tabs["lo"], tabs["nch"], tabs["g"], tabs["tot"], tabs["tail_lo"], tabs["tail_n"], u2, pos)


def _experts_kernel(te_tab, nv_tab, xs_ref, w1_ref, w3_ref, w2_ref, ys_ref):
    @pl.when(pl.program_id(0) < nv_tab[0])
    def _():
        x = xs_ref[...]
        h1 = jnp.dot(x, w1_ref[0].astype(BF16), preferred_element_type=F32)
        h3 = jnp.dot(x, w3_ref[0].astype(BF16), preferred_element_type=F32)
        h = (_silu(h1) * h3).astype(BF16)
        ys_ref[...] = jnp.dot(h, w2_ref[0].astype(BF16), preferred_element_type=F32).astype(BF16)


def _experts(xs, w1, w3, w2, tile_expert, n_valid):
    rows, d = xs.shape
    de = w1.shape[2]
    row = pl.BlockSpec((EXPERT_ROW_TILE, d), lambda j, te, nv: (jnp.minimum(j, nv[0] - 1), 0))
    return pl.pallas_call(
        _experts_kernel,
        out_shape=jax.ShapeDtypeStruct((rows, d), BF16),
        grid_spec=pltpu.PrefetchScalarGridSpec(
            num_scalar_prefetch=2,
            grid=(rows // EXPERT_ROW_TILE,),
            in_specs=[row,
                      pl.BlockSpec((1, d, de), lambda j, te, nv: (te[j], 0, 0)),
                      pl.BlockSpec((1, d, de), lambda j, te, nv: (te[j], 0, 0)),
                      pl.BlockSpec((1, de, d), lambda j, te, nv: (te[j], 0, 0))],
            out_specs=row,
        ),
        compiler_params=_params(("arbitrary",)),
        name="experts",
    )(tile_expert, n_valid, xs, w1, w3, w2)


def _combine_kernel(lo_tab, nch_tab, g_tab, tot_tab,
                    ys_hbm, pwt_ref, x1_ref, u2_ref, mod_ref, sw1_ref, sw3_ref, sw2_ref, g_ref, b_ref,
                    o_ref, loc, acc, sem, *, n_batch, n_ctx_tiles, d, alpha):
    i = _flat_tile()
    tot = tot_tab[i]

    @pl.when(i == 0)
    def _():
        loc[...] = jnp.zeros_like(loc)

    _move_tile_chunks(i, lo_tab, nch_tab, g_tab, loc, ys_hbm, sem, to_global=False)

    u = u2_ref[0]
    hs = (_silu(jnp.dot(u, sw1_ref[...], preferred_element_type=F32))
          * jnp.dot(u, sw3_ref[...], preferred_element_type=F32)).astype(BF16)
    acc[...] = jnp.dot(hs, sw2_ref[...], preferred_element_type=F32)

    _wait_chunks(tot, ys_hbm, loc, sem)

    pw = pwt_ref[...]
    half = ROW_TILE // 2
    pos_b = [jnp.broadcast_to(pw[:, k:k + 1], (ROW_TILE, half)) for k in range(TOP_K)]
    wts_b = [jnp.broadcast_to(pw[:, TOP_K + k:TOP_K + k + 1], (ROW_TILE, half)) for k in range(TOP_K)]
    cid0 = lax.broadcasted_iota(jnp.int32, (ROW_TILE, half), 1).astype(F32)
    n_blk = (tot * CHUNK + ROW_TILE - 1) // ROW_TILE

    def gather(rb, carry):
        r0 = pl.multiple_of(rb * ROW_TILE, ROW_TILE)
        parts = []
        for hf in range(2):
            cid = cid0 + (r0 + hf * half).astype(F32)
            m = jnp.zeros((ROW_TILE, half), F32)
            for k in range(TOP_K):
                m = jnp.where(cid == pos_b[k], wts_b[k], m)
            parts.append(m.astype(BF16))
        sel = jnp.concatenate(parts, axis=1)
        acc[...] += jnp.dot(sel, loc[pl.ds(r0, ROW_TILE), :], preferred_element_type=F32)
        return carry

    lax.fori_loop(0, n_blk, gather, 0)

    m = _mod_row(mod_ref, n_batch, n_ctx_tiles)
    g2 = m[:, 5 * d:6 * d]
    o_ref[0] = _layer_norm(alpha * x1_ref[0] + g2 * acc[...], g_ref[...], b_ref[...])


def _combine(ys, pwt, x1, u2, mod_l, sw1, sw3, sw2, ln_g, ln_b, tabs, n_ctx, alpha):
    n_batch, t, d = x1.shape
    tiles = t // ROW_TILE
    row = pl.BlockSpec((1, ROW_TILE, d), lambda b, j, *_: (b, j, 0))
    return pl.pallas_call(
        functools.partial(_combine_kernel, n_batch=n_batch, n_ctx_tiles=n_ctx // ROW_TILE, d=d,
                          alpha=alpha),
        out_shape=jax.ShapeDtypeStruct((n_batch, t, d), F32),
        grid_spec=pltpu.PrefetchScalarGridSpec(
            num_scalar_prefetch=4,
            grid=(n_batch, tiles),
            in_specs=[pl.BlockSpec(memory_space=pl.ANY),
                      pl.BlockSpec((ROW_TILE, 128), lambda b, j, *_: (b * tiles + j, 0)),
                      row, row] + [_const_spec(a.shape) for a in (mod_l, sw1, sw3, sw2, ln_g, ln_b)],
            out_specs=row,
            scratch_shapes=[pltpu.VMEM((LOCAL_ROWS, d), BF16), pltpu.VMEM((ROW_TILE, d), F32),
                            pltpu.SemaphoreType.DMA],
        ),
        compiler_params=_params(("arbitrary", "arbitrary")),
        name="combine",
    )(tabs["lo"], tabs["nch"], tabs["g"], tabs["tot"], ys, pwt, x1, u2, mod_l, sw1, sw3, sw2, ln_g, ln_b)


def _dispatch_tables(nch, lo, n_row_tiles):
    cpt = EXPERT_ROW_TILE // CHUNK
    tot_e = jnp.sum(nch, axis=0)
    region = (tot_e + cpt - 1) // cpt * cpt
    ends = jnp.cumsum(region)
    base = ends - region
    g = base[None, :] + jnp.cumsum(nch, axis=0) - nch
    tile_expert = jnp.searchsorted(ends // cpt, jnp.arange(n_row_tiles, dtype=jnp.int32), side="right")
    i32 = lambda a: a.astype(jnp.int32)
    tabs = dict(lo=i32(lo.reshape(-1)), nch=i32(nch.reshape(-1)), g=i32(g.reshape(-1)),
                tot=i32(jnp.sum(nch, axis=1)), tail_lo=i32(base + tot_e), tail_n=i32(region - tot_e))
    return tabs, i32(jnp.minimum(tile_expert, N_EXPERTS - 1)), i32(ends[-1:] // cpt)


def _rope_tables(n_ctx, n_lat):
    pos = jnp.arange(n_lat)
    n_freq = HEAD_DIM // 4
    inv = ROPE_THETA ** (-jnp.arange(n_freq, dtype=F32) / n_freq)
    ang_r = (pos // GRID_W).astype(F32)[:, None] * inv
    ang_c = (pos % GRID_W).astype(F32)[:, None] * inv
    cos_l = jnp.concatenate([jnp.cos(ang_r)] * 2 + [jnp.cos(ang_c)] * 2, axis=1)
    sin_l = jnp.concatenate([-jnp.sin(ang_r), jnp.sin(ang_r), -jnp.sin(ang_c), jnp.sin(ang_c)], axis=1)
    cos_t = jnp.concatenate([jnp.ones((n_ctx, HEAD_DIM), F32), cos_l], axis=0)
    sin_t = jnp.concatenate([jnp.zeros((n_ctx, HEAD_DIM), F32), sin_l], axis=0)
    return cos_t, sin_t


def _pack_gate_weights(wa, wi):
    n_dir, n_blocks, w, _ = wa.shape
    per_tile = MXU_TILE // w
    n_tiles = n_blocks // per_tile

    def dense(wx):
        wx = wx.reshape(n_dir, n_tiles, per_tile, w, w)
        eye = jnp.eye(per_tile, dtype=wx.dtype)
        full = jnp.einsum('dtpij,pq->dtpiqj', wx, eye)
        return full.reshape(n_dir, n_tiles, MXU_TILE, MXU_TILE)

    return jnp.concatenate([dense(wa), dense(wi)], axis=-1).astype(BF16)


def kernel(x, c, ctx, c_ctx, w_mod, b_mod, w_in, sink, conv_w, conv_b, rg_wa, rg_ba, rg_wi, rg_bi,
           rg_lambda, w_o_attn, w_o_rnn, w_out, ln1_g, ln1_b, router_w, router_bias, exp_w1, exp_w3,
           exp_w2, sh_w1, sh_w3, sh_w2, ln2_g, ln2_b):
    n_batch, n_lat, d = x.shape
    n_ctx = ctx.shape[1]
    n_layers = w_mod.shape[0]
    t = n_ctx + n_lat
    assert n_ctx % ROW_TILE == 0 and n_lat % ROW_TILE == 0 and n_batch + 1 <= MOD_ROWS
    assert exp_w1.shape[1:] == (N_EXPERTS, d, D_EXPERT) and sh_w1.shape[1:] == (d, D_EXPERT)
    alpha = (2 * n_layers) ** 0.25
    n_tiles = n_batch * t // ROW_TILE
    chunks_per_row_tile = EXPERT_ROW_TILE // CHUNK
    max_chunks = (TOP_K * n_batch * t // CHUNK + n_tiles * N_EXPERTS
                  + N_EXPERTS * (chunks_per_row_tile - 1))
    n_row_tiles = pl.cdiv(max_chunks, chunks_per_row_tile)

    xa = jnp.concatenate([ctx, x], axis=1)
    cc = jnp.zeros((MOD_ROWS, d), F32).at[:n_batch].set(c).at[n_batch].set(c_ctx)
    mod = _modulation(cc, w_mod, b_mod)
    cos_t, sin_t = _rope_tables(n_ctx, n_lat)

    for l in range(n_layers):
        mod_l = mod[l]
        q, k, v, xr, gy, sa, sr = _in_projection(xa, mod_l, w_in[l].astype(BF16), cos_t, sin_t, n_ctx)
        attn = _attention(sink[l], q, k, v, n_ctx)
        wg = _pack_gate_weights(rg_wa[l], rg_wi[l])
        hf, hb = _rnn_branch(xr, conv_w[l], conv_b[l][None, :], wg, rg_ba[l], rg_bi[l], rg_lambda[l],
                             n_ctx)
        rw_t = router_w[l].T
        rw_hi = rw_t.astype(BF16)
        rw_lo = (rw_t - rw_hi.astype(F32)).astype(BF16)
        x1, u2, logits_t = _out_projection(
            xa, attn, hf, hb, gy, sa, sr, mod_l, w_o_attn[l].astype(BF16), w_o_rnn[l].astype(BF16),
            w_out[l].astype(BF16), ln1_g[l][None, :], ln1_b[l][None, :], rw_hi, rw_lo, n_ctx, alpha)
        pos, pwt, nch, lo = _route(logits_t, router_bias[l][:, None])
        tabs, tile_expert, n_valid = _dispatch_tables(nch[:, :, 0], lo[:, :, 0], n_row_tiles)
        xs = _dispatch(u2, pos, tabs, n_row_tiles)
        ys = _experts(xs, exp_w1[l], exp_w3[l], exp_w2[l], tile_expert, n_valid)
        xa = _combine(ys, pwt, x1, u2, mod_l, sh_w1[l].astype(BF16), sh_w3[l].astype(BF16),
                      sh_w2[l].astype(BF16), ln2_g[l][None, :], ln2_b[l][None, :], tabs, n_ctx, alpha)
    return xa[:, n_ctx:, :]
```

```python
import functools
import math

import jax
import jax.numpy as jnp
from jax import lax
from jax.experimental import pallas as pl
from jax.experimental.pallas import tpu as pltpu

N_HEADS = 8
N_KV_HEADS = 2
HEAD_DIM = 128
KV_GROUP = N_HEADS // N_KV_HEADS
ATT_BLOCK = 128
GRID_W = 64
ROPE_THETA = 10000.0
RNN_BLOCKS = 16
LRU_C = 8.0
N_EXPERTS = 64
TOP_K = 8
N_GROUPS = 8
TOPK_GROUPS = 4
D_EXPERT = 256
ROUTE_SCALE = 2.5
LN_EPS = 1e-6

ROW_TILE = 256
MXU_TILE = 256
CHUNK = 16
LOCAL_ROWS = TOP_K * ROW_TILE + N_EXPERTS * CHUNK
MAX_CHUNKS = LOCAL_ROWS // CHUNK
EXPERT_ROW_TILE = 512
ISSUE_UNROLL = 4
WAIT_GROUP = 16
MOD_ROWS = 24
VMEM_LIMIT = 56 * 1024 * 1024

F32 = jnp.float32
BF16 = jnp.bfloat16
NEG_BIG = -1e30


def _const_spec(shape):
    zeros = (0,) * len(shape)
    return pl.BlockSpec(shape, lambda *_: zeros, pipeline_mode=pl.Buffered(1))


def _params(sem):
    return pltpu.CompilerParams(dimension_semantics=sem, vmem_limit_bytes=VMEM_LIMIT)


def _silu(v):
    return v * jax.nn.sigmoid(v)


def _gelu_tanh(v):
    return v * (0.5 * (1.0 + jnp.tanh(math.sqrt(2.0 / math.pi) * (v + 0.044715 * (v * v * v)))))


def _layer_norm(v, g, b):
    mu = jnp.mean(v, axis=-1, keepdims=True)
    d = v - mu
    var = jnp.mean(d * d, axis=-1, keepdims=True)
    return d * lax.rsqrt(var + LN_EPS) * g + b


def _mod_kernel(cc_ref, w_ref, b_ref, o_ref):
    s = _silu(cc_ref[...]).astype(BF16)
    o_ref[0] = jnp.dot(s, w_ref[0].astype(BF16), preferred_element_type=F32) + b_ref[0]


def _modulation(cc, w_mod, b_mod):
    n_layers, d, d6 = w_mod.shape
    col = 1536
    return pl.pallas_call(
        _mod_kernel,
        out_shape=jax.ShapeDtypeStruct((n_layers, MOD_ROWS, d6), F32),
        grid=(n_layers, d6 // col),
        in_specs=[
            pl.BlockSpec((MOD_ROWS, d), lambda l, j: (0, 0)),
            pl.BlockSpec((1, d, col), lambda l, j: (l, 0, j)),
            pl.BlockSpec((1, 1, col), lambda l, j: (l, 0, j)),
        ],
        out_specs=pl.BlockSpec((1, MOD_ROWS, col), lambda l, j: (l, 0, j)),
        compiler_params=_params(("parallel", "parallel")),
        name="modulation",
    )(cc, w_mod, b_mod.reshape(n_layers, 1, d6))


def _mod_row(mod_ref, n_batch, n_ctx_tiles):
    b = pl.program_id(0)
    j = pl.program_id(1)
    row = jnp.where(j < n_ctx_tiles, n_batch, b)
    return mod_ref[pl.ds(row, 1), :]


def _rope(v, cos_w, sin_w):
    width = v.shape[1]
    lane = lax.broadcasted_iota(jnp.int32, v.shape, 1)
    partner = jnp.where((lane & 32) == 0, pltpu.roll(v, width - 32, 1), pltpu.roll(v, 32, 1))
    return v * cos_w + partner * sin_w


def _inproj_kernel(x_ref, mod_ref, w_ref, cos_ref, sin_ref,
                   q_ref, k_ref, v_ref, xr_ref, gy_ref, sa_ref, sr_ref,
                   *, n_batch, n_ctx_tiles, d):
    m = _mod_row(mod_ref, n_batch, n_ctx_tiles)
    sh1 = m[:, 0:d]
    sc1 = m[:, d:2 * d]
    u = (x_ref[0] * (1.0 + sc1) + sh1).astype(BF16)

    def proj(lo, hi):
        return jnp.dot(u, w_ref[:, lo:hi], preferred_element_type=F32)

    qw = N_HEADS * HEAD_DIM
    kw = N_KV_HEADS * HEAD_DIM
    cos1 = cos_ref[...]
    sin1 = sin_ref[...]
    o = 0
    q = proj(o, o + qw)
    q_ref[0] = _rope(q, jnp.concatenate([cos1] * N_HEADS, axis=1),
                     jnp.concatenate([sin1] * N_HEADS, axis=1)).astype(BF16)
    o += qw
    k = proj(o, o + kw)
    k_ref[0] = _rope(k, jnp.concatenate([cos1] * N_KV_HEADS, axis=1),
                     jnp.concatenate([sin1] * N_KV_HEADS, axis=1)).astype(BF16)
    o += kw
    v_ref[0] = proj(o, o + kw).astype(BF16)
    o += kw
    xr_ref[0] = proj(o, o + d)
    o += d
    gy_ref[0] = _gelu_tanh(proj(o, o + d)).astype(BF16)
    o += d
    sa_ref[0] = jax.nn.sigmoid(proj(o, o + d)).astype(BF16)
    o += d
    sr_ref[0] = jax.nn.sigmoid(proj(o, o + d)).astype(BF16)


def _in_projection(xa, mod_l, w_in, cos_t, sin_t, n_ctx):
    n_batch, t, d = xa.shape
    qw = N_HEADS * HEAD_DIM
    kw = N_KV_HEADS * HEAD_DIM
    row = lambda w: pl.BlockSpec((1, ROW_TILE, w), lambda b, j: (b, j, 0))
    tab = pl.BlockSpec((ROW_TILE, HEAD_DIM), lambda b, j: (j, 0))
    shp = lambda w, dt: jax.ShapeDtypeStruct((n_batch, t, w), dt)
    return pl.pallas_call(
        functools.partial(_inproj_kernel, n_batch=n_batch, n_ctx_tiles=n_ctx // ROW_TILE, d=d),
        out_shape=(shp(qw, BF16), shp(kw, BF16), shp(kw, BF16), shp(d, F32),
                   shp(d, BF16), shp(d, BF16), shp(d, BF16)),
        grid=(n_batch, t // ROW_TILE),
        in_specs=[row(d), _const_spec(mod_l.shape), _const_spec(w_in.shape), tab, tab],
        out_specs=(row(qw), row(kw), row(kw), row(d), row(d), row(d), row(d)),
        compiler_params=_params(("parallel", "parallel")),
        name="in_projection",
    )(xa, mod_l, w_in, cos_t, sin_t)


def _attn_kernel(sink_ref, q_ref, kp_ref, kc_ref, kn_ref, kx_ref, vp_ref, vc_ref, vn_ref, vx_ref,
                 o_ref, *, n_ctx_blocks, n_lat_blocks):
    i = pl.program_id(1) - n_ctx_blocks
    n_ctx = kx_ref.shape[1]
    n_win = 3 * ATT_BLOCK
    n_keys = n_win + n_ctx
    rows = KV_GROUP * ATT_BLOCK
    rr = lax.broadcasted_iota(jnp.int32, (rows, n_keys), 0) & (ATT_BLOCK - 1)
    jj = lax.broadcasted_iota(jnp.int32, (rows, n_keys), 1)
    in_band = (jj >= rr) & (jj <= rr + 2 * ATT_BLOCK)
    left_ok = (i > 0) | (jj >= ATT_BLOCK)
    right_ok = (i < n_lat_blocks - 1) | (jj < 2 * ATT_BLOCK)
    allowed = (jj >= n_win) | (in_band & left_ok & right_ok & (i >= 0))
    rblk = lax.broadcasted_iota(jnp.int32, (rows, 1), 0) // ATT_BLOCK
    scale = HEAD_DIM ** -0.5
    q = q_ref[0]
    for h in range(N_KV_HEADS):
        hs = slice(h * HEAD_DIM, (h + 1) * HEAD_DIM)
        kk = jnp.concatenate([kp_ref[0, :, hs], kc_ref[0, :, hs], kn_ref[0, :, hs], kx_ref[0, :, hs]], axis=0)
        vv = jnp.concatenate([vp_ref[0, :, hs], vc_ref[0, :, hs], vn_ref[0, :, hs], vx_ref[0, :, hs]], axis=0)
        heads = [h * KV_GROUP + g for g in range(KV_GROUP)]
        qs = jnp.concatenate([q[:, hd * HEAD_DIM:(hd + 1) * HEAD_DIM] for hd in heads], axis=0)
        s = lax.dot_general(qs, kk, (((1,), (1,)), ((), ())), preferred_element_type=F32) * scale
        s = jnp.where(allowed, s, NEG_BIG)
        sink = jnp.zeros((rows, 1), F32)
        for g, hd in enumerate(heads):
            sink = jnp.where(rblk == g, sink_ref[hd], sink)
        mx = jnp.maximum(jnp.max(s, axis=1, keepdims=True), sink)
        p = jnp.exp(s - mx)
        den = jnp.sum(p, axis=1, keepdims=True) + jnp.exp(sink - mx)
        o = jnp.dot(p.astype(BF16), vv, preferred_element_type=F32) / den
        for g, hd in enumerate(heads):
            o_ref[0, :, hd * HEAD_DIM:(hd + 1) * HEAD_DIM] = o[g * ATT_BLOCK:(g + 1) * ATT_BLOCK].astype(BF16)


def _attention(sink_l, q, k, v, n_ctx):
    n_batch, t, qw = q.shape
    kw = k.shape[2]
    ncb = n_ctx // ATT_BLOCK
    nlb = (t - n_ctx) // ATT_BLOCK

    def lat_blk(off):
        def index(b, j, sink):
            return (b, ncb + jnp.clip(j - ncb + off, 0, nlb - 1), 0)
        return pl.BlockSpec((1, ATT_BLOCK, kw), index)

    ctx_blk = pl.BlockSpec((1, n_ctx, kw), lambda b, j, sink: (b, 0, 0))
    q_blk = pl.BlockSpec((1, ATT_BLOCK, qw), lambda b, j, sink: (b, j, 0))
    kv_specs = [lat_blk(-1), lat_blk(0), lat_blk(1), ctx_blk]
    return pl.pallas_call(
        functools.partial(_attn_kernel, n_ctx_blocks=ncb, n_lat_blocks=nlb),
        out_shape=jax.ShapeDtypeStruct((n_batch, t, qw), BF16),
        grid_spec=pltpu.PrefetchScalarGridSpec(
            num_scalar_prefetch=1,
            grid=(n_batch, t // ATT_BLOCK),
            in_specs=[q_blk] + kv_specs + kv_specs,
            out_specs=q_blk,
        ),
        compiler_params=_params(("parallel", "parallel")),
        name="attention",
    )(sink_l, q, k, k, k, k, v, v, v, v)


def _rnn_kernel(x_ref, cw_ref, cb_ref, wg_ref, ba_ref, bi_ref, lam_ref, hf_ref, hb_ref,
                a_scr, b_scr, h_scr, *, n_ctx, n_rows):
    j = pl.program_id(1)
    tb = ROW_TILE
    n_blk = n_rows // tb
    n_cblk = n_ctx // tb
    blk_f = j
    blk_b = jnp.where(j < n_cblk, n_cblk - 1 - j, n_blk - 1 - (j - n_cblk))
    d = x_ref.shape[2]
    ridx = lax.broadcasted_iota(jnp.int32, (tb, d), 0)

    def conv_block(blk):
        t0 = pl.multiple_of(blk * tb, tb)
        cur = x_ref[0, pl.ds(t0, tb), :]
        prev8 = x_ref[0, pl.ds(pl.multiple_of(jnp.maximum(t0 - 8, 0), 8), 8), :]
        next8 = x_ref[0, pl.ds(pl.multiple_of(jnp.minimum(t0 + tb, n_rows - 8), 8), 8), :]
        prev_ok = (t0 != 0) & (t0 != n_ctx)
        next_ok = (t0 + tb != n_ctx) & (t0 + tb != n_rows)
        prev8 = jnp.where(prev_ok, prev8, 0.0)
        next8 = jnp.where(next_ok, next8, 0.0)
        p6 = jnp.broadcast_to(prev8[6:7, :], (tb, d))
        p7 = jnp.broadcast_to(prev8[7:8, :], (tb, d))
        n0 = jnp.broadcast_to(next8[0:1, :], (tb, d))
        xm1 = jnp.where(ridx == 0, p7, pltpu.roll(cur, 1, 0))
        xm2 = jnp.where(ridx == 0, p6, jnp.where(ridx == 1, p7, pltpu.roll(cur, 2, 0)))
        xp1 = jnp.where(ridx == tb - 1, n0, pltpu.roll(cur, tb - 1, 0))
        return (cb_ref[...] + xm2 * cw_ref[0:1, :] + xm1 * cw_ref[1:2, :]
                + cur * cw_ref[2:3, :] + xp1 * cw_ref[3:4, :])

    def gate_block(xc, dr):
        xb = xc.astype(BF16)
        r_parts, i_parts = [], []
        for jt in range(d // MXU_TILE):
            g = jnp.dot(xb[:, jt * MXU_TILE:(jt + 1) * MXU_TILE], wg_ref[dr, jt],
                        preferred_element_type=F32)
            r_parts.append(g[:, :MXU_TILE])
            i_parts.append(g[:, MXU_TILE:])
        r = jax.nn.sigmoid(jnp.concatenate(r_parts, axis=1) + ba_ref[dr:dr + 1, :])
        ig = jax.nn.sigmoid(jnp.concatenate(i_parts, axis=1) + bi_ref[dr:dr + 1, :])
        nl = -lam_ref[dr:dr + 1, :]
        softplus = jnp.maximum(nl, 0.0) + jnp.log1p(jnp.exp(-jnp.abs(nl)))
        log_a = (-LRU_C) * r * softplus
        a = jnp.exp(log_a)
        a_scr[dr] = a
        b_scr[dr] = jnp.sqrt(1.0 - a * a) * (ig * xc)

    gate_block(conv_block(blk_f), 0)
    gate_block(conv_block(blk_b), 1)

    @pl.when(j == 0)
    def _():
        h_scr[...] = jnp.zeros_like(h_scr)

    def step(s, carry):
        hf, hb = carry
        tf = s
        tr = tb - 1 - s
        hf = a_scr[0, pl.ds(tf, 1), :] * hf + b_scr[0, pl.ds(tf, 1), :]
        hb = a_scr[1, pl.ds(tr, 1), :] * hb + b_scr[1, pl.ds(tr, 1), :]
        hf_ref[0, pl.ds(tf, 1), :] = hf
        hb_ref[0, pl.ds(tr, 1), :] = hb
        return hf, hb

    hf, hb = lax.fori_loop(0, tb, step, (h_scr[0:1, :], h_scr[1:2, :]), unroll=8)
    h_scr[0:1, :] = hf
    h_scr[1:2, :] = hb


def _rnn_branch(xr, conv_w, conv_b, wg, ba, bi, lam, n_ctx):
    n_batch, t, d = xr.shape
    n_blk = t // ROW_TILE
    n_cblk = n_ctx // ROW_TILE

    def bwd_index(b, j):
        return (b, jnp.where(j < n_cblk, n_cblk - 1 - j, n_blk - 1 - (j - n_cblk)), 0)

    out = jax.ShapeDtypeStruct((n_batch, t, d), F32)
    return pl.pallas_call(
        functools.partial(_rnn_kernel, n_ctx=n_ctx, n_rows=t),
        out_shape=(out, out),
        grid=(n_batch, n_blk),
        in_specs=[
            pl.BlockSpec((1, t, d), lambda b, j: (b, 0, 0)),
            _const_spec(conv_w.shape), _const_spec(conv_b.shape), _const_spec(wg.shape),
            _const_spec(ba.shape), _const_spec(bi.shape), _const_spec(lam.shape),
        ],
        out_specs=(pl.BlockSpec((1, ROW_TILE, d), lambda b, j: (b, j, 0)),
                   pl.BlockSpec((1, ROW_TILE, d), bwd_index)),
        scratch_shapes=[pltpu.VMEM((2, ROW_TILE, d), F32), pltpu.VMEM((2, ROW_TILE, d), F32),
                        pltpu.VMEM((8, d), F32)],
        compiler_params=_params(("parallel", "arbitrary")),
        name="rnn_branch",
    )(xr, conv_w, conv_b, wg, ba, bi, lam)


def _dot_nt(a, b):
    return lax.dot_general(a, b, (((1,), (1,)), ((), ())), preferred_element_type=F32)


def _outproj_kernel(x_ref, attn_ref, hf_ref, hb_ref, gy_ref, sa_ref, sr_ref, mod_ref,
                    woa_ref, wor_ref, wout_ref, g_ref, b_ref, rwh_ref, rwl_ref,
                    x1_ref, u2_ref, lg_ref, *, n_batch, n_ctx_tiles, d, alpha):
    m = _mod_row(mod_ref, n_batch, n_ctx_tiles)
    g1 = m[:, 2 * d:3 * d]
    sh2 = m[:, 3 * d:4 * d]
    sc2 = m[:, 4 * d:5 * d]
    a = jnp.dot(attn_ref[0], woa_ref[...], preferred_element_type=F32)
    rnn = ((hf_ref[0] + hb_ref[0]) * gy_ref[0].astype(F32)).astype(BF16)
    r = jnp.dot(rnn, wor_ref[...], preferred_element_type=F32)
    merged = (sa_ref[0].astype(F32) * a + sr_ref[0].astype(F32) * r).astype(BF16)
    mix = jnp.dot(merged, wout_ref[...], preferred_element_type=F32)
    x1 = _layer_norm(alpha * x_ref[0] + g1 * mix, g_ref[...], b_ref[...])
    x1_ref[0] = x1
    u2 = x1 * (1.0 + sc2) + sh2
    u_hi = u2.astype(BF16)
    u2_ref[0] = u_hi
    u_lo = (u2 - u_hi.astype(F32)).astype(BF16)
    lg_ref[...] = (_dot_nt(rwh_ref[...], u_hi) + _dot_nt(rwh_ref[...], u_lo)
                   + _dot_nt(rwl_ref[...], u_hi))


def _out_projection(xa, attn, hf, hb, gy, sa, sr, mod_l, woa, wor, wout, ln_g, ln_b, rwh, rwl,
                    n_ctx, alpha):
    n_batch, t, d = xa.shape
    tiles = t // ROW_TILE
    row = pl.BlockSpec((1, ROW_TILE, d), lambda b, j: (b, j, 0))
    return pl.pallas_call(
        functools.partial(_outproj_kernel, n_batch=n_batch, n_ctx_tiles=n_ctx // ROW_TILE, d=d,
                          alpha=alpha),
        out_shape=(jax.ShapeDtypeStruct((n_batch, t, d), F32),
                   jax.ShapeDtypeStruct((n_batch, t, d), BF16),
                   jax.ShapeDtypeStruct((N_EXPERTS, n_batch * t), F32)),
        grid=(n_batch, tiles),
        in_specs=[row] * 7 + [_const_spec(a.shape) for a in
                              (mod_l, woa, wor, wout, ln_g, ln_b, rwh, rwl)],
        out_specs=(row, row,
                   pl.BlockSpec((N_EXPERTS, ROW_TILE), lambda b, j: (0, b * tiles + j))),
        compiler_params=_params(("parallel", "parallel")),
        name="out_projection",
    )(xa, attn, hf, hb, gy, sa, sr, mod_l, woa, wor, wout, ln_g, ln_b, rwh, rwl)


def _route_kernel(lg_ref, bias_ref, pos_ref, wts_ref, nch_ref, lo_ref):
    lg = lg_ref[...]
    n_tok = lg.shape[1]
    per_group = N_EXPERTS // N_GROUPS
    scores = jax.nn.sigmoid(lg)
    biased = scores + bias_ref[...]
    b3 = biased.reshape(N_GROUPS, per_group, n_tok)
    sub = lax.broadcasted_iota(jnp.int32, b3.shape, 1)
    m1 = jnp.max(b3, axis=1, keepdims=True)
    i1 = jnp.min(jnp.where(b3 == m1, sub, per_group), axis=1, keepdims=True)
    m2 = jnp.max(jnp.where(sub == i1, -jnp.inf, b3), axis=1, keepdims=True)
    grp = (m1 + m2).reshape(N_GROUPS, n_tok)

    gi = lax.broadcasted_iota(jnp.int32, grp.shape, 0)
    gsel = jnp.zeros(grp.shape, F32)
    for _ in range(TOPK_GROUPS):
        m = jnp.max(grp, axis=0, keepdims=True)
        idx = jnp.min(jnp.where(grp == m, gi, N_GROUPS), axis=0, keepdims=True)
        hit = gi == idx
        gsel = jnp.where(hit, 1.0, gsel)
        grp = jnp.where(hit, -jnp.inf, grp)
    emask = jnp.broadcast_to(gsel.reshape(N_GROUPS, 1, n_tok), b3.shape).reshape(N_EXPERTS, n_tok)
    cand = jnp.where(emask > 0.0, biased, -jnp.inf)

    ei = lax.broadcasted_iota(jnp.int32, cand.shape, 0)
    comb = jnp.zeros(cand.shape, F32)
    picked = jnp.zeros(cand.shape, F32)
    hits = []
    for _ in range(TOP_K):
        m = jnp.max(cand, axis=0, keepdims=True)
        idx = jnp.min(jnp.where(cand == m, ei, N_EXPERTS), axis=0, keepdims=True)
        hit = ei == idx
        hits.append(hit)
        comb = jnp.where(hit, scores, comb)
        picked = jnp.where(hit, 1.0, picked)
        cand = jnp.where(hit, -jnp.inf, cand)
    comb = comb / jnp.sum(comb, axis=0, keepdims=True) * ROUTE_SCALE

    cnt = jnp.sum(picked, axis=1, keepdims=True)
    nch = jnp.broadcast_to(jnp.floor((cnt + (CHUNK - 1)) * (1.0 / CHUNK)), (N_EXPERTS, 128))
    erow = lax.broadcasted_iota(jnp.int32, nch.shape, 0)
    incl = nch
    for s in (1, 2, 4, 8, 16, 32):
        incl = incl + jnp.where(erow >= s, pltpu.roll(incl, s, 0), 0.0)
    lo = incl - nch
    earlier = (lax.broadcasted_iota(jnp.int32, (n_tok, n_tok), 0)
               < lax.broadcasted_iota(jnp.int32, (n_tok, n_tok), 1)).astype(BF16)
    rank = jnp.dot(picked.astype(BF16), earlier, preferred_element_type=F32)
    slot = lo[:, 0:1] * CHUNK + rank
    pos = jnp.concatenate([jnp.sum(jnp.where(h, slot, 0.0), axis=0, keepdims=True) for h in hits], axis=0)
    wts = jnp.concatenate([jnp.sum(jnp.where(h, comb, 0.0), axis=0, keepdims=True) for h in hits], axis=0)
    pos_ref[...] = pos.astype(jnp.int32)
    wts_ref[...] = wts
    nch_ref[0] = nch.astype(jnp.int32)
    lo_ref[0] = lo.astype(jnp.int32)


def _route(logits_t, bias_col):
    n_tok = logits_t.shape[1]
    n_tiles = n_tok // ROW_TILE
    tab = jax.ShapeDtypeStruct((n_tiles, N_EXPERTS, 128), jnp.int32)
    tab_spec = pl.BlockSpec((1, N_EXPERTS, 128), lambda i: (i, 0, 0))
    pair_spec = pl.BlockSpec((TOP_K, ROW_TILE), lambda i: (0, i))
    return pl.pallas_call(
        _route_kernel,
        out_shape=(jax.ShapeDtypeStruct((TOP_K, n_tok), jnp.int32),
                   jax.ShapeDtypeStruct((TOP_K, n_tok), F32), tab, tab),
        grid=(n_tiles,),
        in_specs=[pl.BlockSpec((N_EXPERTS, ROW_TILE), lambda i: (0, i)), _const_spec(bias_col.shape)],
        out_specs=(pair_spec, pair_spec, tab_spec, tab_spec),
        compiler_params=_params(("parallel",)),
        name="route",
    )(logits_t, bias_col)


def _chunk_copy(src_ref, src_chunk, dst_ref, dst_chunk, sem):
    return pltpu.make_async_copy(src_ref.at[src_chunk], dst_ref.at[dst_chunk], sem)


def _move_tile_chunks(tile, dst_tab, tot_tab, local_ref, global_ref, sem, to_global):
    def one(c):
        g = dst_tab[tile * MAX_CHUNKS + c]
        if to_global:
            _chunk_copy(local_ref, c, global_ref, g, sem).start()
        else:
            _chunk_copy(global_ref, g, local_ref, c, sem).start()

    def group(q, carry):
        for u in range(ISSUE_UNROLL):
            one(q * ISSUE_UNROLL + u)
        return carry

    def single(c, carry):
        one(c)
        return carry

    tot = tot_tab[tile]
    n_groups = tot // ISSUE_UNROLL
    lax.fori_loop(0, n_groups, group, 0)
    lax.fori_loop(n_groups * ISSUE_UNROLL, tot, single, 0)


def _wait_chunks(n, src_ref, dst_ref, sem):
    def group(q, carry):
        pltpu.make_async_copy(src_ref.at[pl.ds(0, WAIT_GROUP)], dst_ref.at[pl.ds(0, WAIT_GROUP)],
                              sem).wait()
        return carry

    def single(c, carry):
        _chunk_copy(src_ref, 0, dst_ref, 0, sem).wait()
        return carry

    n_groups = n // WAIT_GROUP
    lax.fori_loop(0, n_groups, group, 0)
    lax.fori_loop(n_groups * WAIT_GROUP, n, single, 0)


def _flat_tile():
    return pl.program_id(0) * pl.num_programs(1) + pl.program_id(1)


def _dispatch_kernel(dst_tab, tot_tab, tail_lo, tail_n,
                     x_ref, pos_ref, wts_ref, xs_hbm, sel_ref, loc, zbuf, sems):
    i = _flat_tile()
    n_tiles = pl.num_programs(0) * pl.num_programs(1)
    slot = i % 2
    pos = pos_ref[...]
    wts = wts_ref[...]
    x = x_ref[0]
    rid0 = lax.broadcasted_iota(jnp.int32, (ROW_TILE, ROW_TILE), 0)
    for rb in range(LOCAL_ROWS // ROW_TILE):
        rid = rid0 + rb * ROW_TILE
        pw = jnp.zeros((ROW_TILE, ROW_TILE), F32)
        for k in range(TOP_K):
            pw = jnp.where(rid == pos[k:k + 1, :], wts[k:k + 1, :], pw)
        p = jnp.where(pw > 0.0, 1.0, 0.0).astype(BF16)
        cpb = ROW_TILE // CHUNK
        loc[slot, rb * cpb:(rb + 1) * cpb] = jnp.dot(
            p, x, preferred_element_type=F32).astype(BF16).reshape(cpb, CHUNK, x.shape[1])
        sel_ref[0, rb] = pw.T.astype(BF16)

    _move_tile_chunks(i, dst_tab, tot_tab, loc.at[slot], xs_hbm, sems.at[slot], to_global=True)

    @pl.when(i > 0)
    def _():
        _wait_chunks(tot_tab[i - 1], loc.at[1 - slot], xs_hbm, sems.at[1 - slot])

    @pl.when(i == n_tiles - 1)
    def _():
        _wait_chunks(tot_tab[i], loc.at[slot], xs_hbm, sems.at[slot])
        zbuf[...] = jnp.zeros_like(zbuf)

        def per_expert(e, n_started):
            def per_chunk(c, carry):
                _chunk_copy(zbuf, 0, xs_hbm, tail_lo[e] + c, sems.at[slot]).start()
                return carry

            lax.fori_loop(0, tail_n[e], per_chunk, 0)
            return n_started + tail_n[e]

        n_started = lax.fori_loop(0, N_EXPERTS, per_expert, 0)

        def drain(c, carry):
            _chunk_copy(zbuf, 0, xs_hbm, 0, sems.at[slot]).wait()
            return carry

        lax.fori_loop(0, n_started, drain, 0)


def _dispatch(u2, pos, wts, tabs, n_row_tiles):
    n_batch, t, d = u2.shape
    tiles = t // ROW_TILE
    n_blocks = LOCAL_ROWS // ROW_TILE
    pair = pl.BlockSpec((TOP_K, ROW_TILE), lambda b, j, *_: (0, b * tiles + j))
    return pl.pallas_call(
        _dispatch_kernel,
        out_shape=(jax.ShapeDtypeStruct((n_row_tiles * EXPERT_ROW_TILE // CHUNK, CHUNK, d), BF16),
                   jax.ShapeDtypeStruct((n_batch * tiles, n_blocks, ROW_TILE, ROW_TILE), BF16)),
        grid_spec=pltpu.PrefetchScalarGridSpec(
            num_scalar_prefetch=4,
            grid=(n_batch, tiles),
            in_specs=[pl.BlockSpec((1, ROW_TILE, d), lambda b, j, *_: (b, j, 0)), pair, pair],
            out_specs=(pl.BlockSpec(memory_space=pl.ANY),
                       pl.BlockSpec((1, n_blocks, ROW_TILE, ROW_TILE),
                                    lambda b, j, *_: (b * tiles + j, 0, 0, 0))),
            scratch_shapes=[pltpu.VMEM((2, MAX_CHUNKS, CHUNK, d), BF16), pltpu.VMEM((1, CHUNK, d), BF16),
                            pltpu.SemaphoreType.DMA((2,))],
        ),
        compiler_params=_params(("arbitrary", "arbitrary")),
        name="dispatch",
    )(tabs["dst"], tabs["tot"], tabs["tail_lo"], tabs["tail_n"], u2, pos, wts)


def _experts_kernel(te_tab, nv_tab, xs_ref, w1_ref, w3_ref, w2_ref, ys_ref, w1b, w3b, w2b):
    @pl.when(pl.program_id(0) < nv_tab[0])
    def _():
        j = pl.program_id(0)

        @pl.when((j == 0) | (te_tab[j] != te_tab[jnp.maximum(j - 1, 0)]))
        def _():
            w1b[...] = w1_ref[0].astype(BF16)
            w3b[...] = w3_ref[0].astype(BF16)
            w2b[...] = w2_ref[0].astype(BF16)

        x = xs_ref[...]
        h1 = jnp.dot(x, w1b[...], preferred_element_type=F32)
        h3 = jnp.dot(x, w3b[...], preferred_element_type=F32)
        h = (_silu(h1) * h3).astype(BF16)
        ys_ref[...] = jnp.dot(h, w2b[...], preferred_element_type=F32).astype(BF16)


def _experts(xs, w1, w3, w2, tile_expert, n_valid):
    rows, d = xs.shape
    de = w1.shape[2]
    row = pl.BlockSpec((EXPERT_ROW_TILE, d), lambda j, te, nv: (jnp.minimum(j, nv[0] - 1), 0))
    return pl.pallas_call(
        _experts_kernel,
        out_shape=jax.ShapeDtypeStruct((rows, d), BF16),
        grid_spec=pltpu.PrefetchScalarGridSpec(
            num_scalar_prefetch=2,
            grid=(rows // EXPERT_ROW_TILE,),
            in_specs=[row,
                      pl.BlockSpec((1, d, de), lambda j, te, nv: (te[j], 0, 0)),
                      pl.BlockSpec((1, d, de), lambda j, te, nv: (te[j], 0, 0)),
                      pl.BlockSpec((1, de, d), lambda j, te, nv: (te[j], 0, 0))],
            out_specs=row,
            scratch_shapes=[pltpu.VMEM((d, de), BF16), pltpu.VMEM((d, de), BF16),
                            pltpu.VMEM((de, d), BF16)],
        ),
        compiler_params=_params(("arbitrary",)),
        name="experts",
    )(tile_expert, n_valid, xs, w1, w3, w2)


def _combine_kernel(dst_tab, tot_tab,
                    ys_hbm, sel_ref, x1_ref, u2_ref, mod_ref, sw1_ref, sw3_ref, sw2_ref, g_ref, b_ref,
                    o_ref, loc, sems, *, n_batch, n_ctx_tiles, d, alpha):
    i = _flat_tile()
    n_tiles = pl.num_programs(0) * pl.num_programs(1)
    slot = i % 2

    @pl.when(i == 0)
    def _():
        loc[...] = jnp.zeros_like(loc)
        _move_tile_chunks(0, dst_tab, tot_tab, loc.at[0], ys_hbm, sems.at[0], to_global=False)

    @pl.when(i + 1 < n_tiles)
    def _():
        _move_tile_chunks(i + 1, dst_tab, tot_tab, loc.at[1 - slot], ys_hbm, sems.at[1 - slot],
                          to_global=False)

    u = u2_ref[0]
    hs = (_silu(jnp.dot(u, sw1_ref[...], preferred_element_type=F32))
          * jnp.dot(u, sw3_ref[...], preferred_element_type=F32)).astype(BF16)
    shared = jnp.dot(hs, sw2_ref[...], preferred_element_type=F32)

    _wait_chunks(tot_tab[i], ys_hbm, loc.at[slot], sems.at[slot])
    sel = jnp.concatenate([sel_ref[0, rb] for rb in range(LOCAL_ROWS // ROW_TILE)], axis=1)
    routed = jnp.dot(sel, loc[slot].reshape(LOCAL_ROWS, d), preferred_element_type=F32)

    m = _mod_row(mod_ref, n_batch, n_ctx_tiles)
    g2 = m[:, 5 * d:6 * d]
    o_ref[0] = _layer_norm(alpha * x1_ref[0] + g2 * (routed + shared), g_ref[...], b_ref[...])


def _combine(ys, sel, x1, u2, mod_l, sw1, sw3, sw2, ln_g, ln_b, tabs, n_ctx, alpha):
    n_batch, t, d = x1.shape
    tiles = t // ROW_TILE
    n_blocks = LOCAL_ROWS // ROW_TILE
    row = pl.BlockSpec((1, ROW_TILE, d), lambda b, j, *_: (b, j, 0))
    return pl.pallas_call(
        functools.partial(_combine_kernel, n_batch=n_batch, n_ctx_tiles=n_ctx // ROW_TILE, d=d,
                          alpha=alpha),
        out_shape=jax.ShapeDtypeStruct((n_batch, t, d), F32),
        grid_spec=pltpu.PrefetchScalarGridSpec(
            num_scalar_prefetch=2,
            grid=(n_batch, tiles),
            in_specs=[pl.BlockSpec(memory_space=pl.ANY),
                      pl.BlockSpec((1, n_blocks, ROW_TILE, ROW_TILE),
                                   lambda b, j, *_: (b * tiles + j, 0, 0, 0)),
                      row, row] + [_const_spec(a.shape) for a in (mod_l, sw1, sw3, sw2, ln_g, ln_b)],
            out_specs=row,
            scratch_shapes=[pltpu.VMEM((2, MAX_CHUNKS, CHUNK, d), BF16), pltpu.SemaphoreType.DMA((2,))],
        ),
        compiler_params=_params(("arbitrary", "arbitrary")),
        name="combine",
    )(tabs["dst"], tabs["tot"], ys, sel, x1, u2, mod_l, sw1, sw3, sw2, ln_g, ln_b)


def _dispatch_tables(nch, lo, n_row_tiles):
    cpt = EXPERT_ROW_TILE // CHUNK
    tot_e = jnp.sum(nch, axis=0)
    region = (tot_e + cpt - 1) // cpt * cpt
    ends = jnp.cumsum(region)
    base = ends - region
    g = base[None, :] + jnp.cumsum(nch, axis=0) - nch
    c = jnp.arange(MAX_CHUNKS, dtype=nch.dtype)[None, :, None]
    covers = (lo[:, None, :] <= c) & (c < (lo + nch)[:, None, :])
    dst = c[:, :, 0] + jnp.sum(jnp.where(covers, (g - lo)[:, None, :], 0), axis=-1)
    row_tile = jnp.arange(n_row_tiles, dtype=nch.dtype)[:, None]
    tile_expert = jnp.sum((ends // cpt)[None, :] <= row_tile, axis=-1)
    i32 = lambda a: a.astype(jnp.int32)
    tabs = dict(dst=i32(dst.reshape(-1)), tot=i32(jnp.sum(nch, axis=1)),
                tail_lo=i32(base + tot_e), tail_n=i32(region - tot_e))
    return tabs, i32(jnp.minimum(tile_expert, N_EXPERTS - 1)), i32(ends[-1:] // cpt)


def _rope_tables(n_ctx, n_lat):
    pos = jnp.arange(n_lat)
    n_freq = HEAD_DIM // 4
    inv = ROPE_THETA ** (-jnp.arange(n_freq, dtype=F32) / n_freq)
    ang_r = (pos // GRID_W).astype(F32)[:, None] * inv
    ang_c = (pos % GRID_W).astype(F32)[:, None] * inv
    cos_l = jnp.concatenate([jnp.cos(ang_r)] * 2 + [jnp.cos(ang_c)] * 2, axis=1)
    sin_l = jnp.concatenate([-jnp.sin(ang_r), jnp.sin(ang_r), -jnp.sin(ang_c), jnp.sin(ang_c)], axis=1)
    cos_t = jnp.concatenate([jnp.ones((n_ctx, HEAD_DIM), F32), cos_l], axis=0)
    sin_t = jnp.concatenate([jnp.zeros((n_ctx, HEAD_DIM), F32), sin_l], axis=0)
    return cos_t, sin_t


def _pack_gate_weights(wa, wi):
    n_dir, n_blocks, w, _ = wa.shape
    per_tile = MXU_TILE // w
    n_tiles = n_blocks // per_tile

    def dense(wx):
        wx = wx.reshape(n_dir, n_tiles, per_tile, w, w)
        eye = jnp.eye(per_tile, dtype=wx.dtype)
        full = jnp.einsum('dtpij,pq->dtpiqj', wx, eye)
        return full.reshape(n_dir, n_tiles, MXU_TILE, MXU_TILE)

    return jnp.concatenate([dense(wa), dense(wi)], axis=-1).astype(BF16)


def kernel(x, c, ctx, c_ctx, w_mod, b_mod, w_in, sink, conv_w, conv_b, rg_wa, rg_ba, rg_wi, rg_bi,
           rg_lambda, w_o_attn, w_o_rnn, w_out, ln1_g, ln1_b, router_w, router_bias, exp_w1, exp_w3,
           exp_w2, sh_w1, sh_w3, sh_w2, ln2_g, ln2_b):
    n_batch, n_lat, d = x.shape
    n_ctx = ctx.shape[1]
    n_layers = w_mod.shape[0]
    t = n_ctx + n_lat
    assert n_ctx % ROW_TILE == 0 and n_lat % ROW_TILE == 0 and n_batch + 1 <= MOD_ROWS
    assert exp_w1.shape[1:] == (N_EXPERTS, d, D_EXPERT) and sh_w1.shape[1:] == (d, D_EXPERT)
    alpha = (2 * n_layers) ** 0.25
    n_tiles = n_batch * t // ROW_TILE
    chunks_per_row_tile = EXPERT_ROW_TILE // CHUNK
    max_chunks = (TOP_K * n_batch * t // CHUNK + n_tiles * N_EXPERTS
                  + N_EXPERTS * (chunks_per_row_tile - 1))
    n_row_tiles = pl.cdiv(max_chunks, chunks_per_row_tile)

    xa = jnp.concatenate([ctx, x], axis=1)
    cc = jnp.zeros((MOD_ROWS, d), F32).at[:n_batch].set(c).at[n_batch].set(c_ctx)
    mod = _modulation(cc, w_mod, b_mod)
    cos_t, sin_t = _rope_tables(n_ctx, n_lat)

    for l in range(n_layers):
        mod_l = mod[l]
        q, k, v, xr, gy, sa, sr = _in_projection(xa, mod_l, w_in[l].astype(BF16), cos_t, sin_t, n_ctx)
        attn = _attention(sink[l], q, k, v, n_ctx)
        wg = _pack_gate_weights(rg_wa[l], rg_wi[l])
        hf, hb = _rnn_branch(xr, conv_w[l], conv_b[l][None, :], wg, rg_ba[l], rg_bi[l], rg_lambda[l],
                             n_ctx)
        rw_t = router_w[l].T
        rw_hi = rw_t.astype(BF16)
        rw_lo = (rw_t - rw_hi.astype(F32)).astype(BF16)
        x1, u2, logits_t = _out_projection(
            xa, attn, hf, hb, gy, sa, sr, mod_l, w_o_attn[l].astype(BF16), w_o_rnn[l].astype(BF16),
            w_out[l].astype(BF16), ln1_g[l][None, :], ln1_b[l][None, :], rw_hi, rw_lo, n_ctx, alpha)
        pos, wts, nch, lo = _route(logits_t, router_bias[l][:, None])
        tabs, tile_expert, n_valid = _dispatch_tables(nch[:, :, 0], lo[:, :, 0], n_row_tiles)
        xs, sel = _dispatch(u2, pos, wts, tabs, n_row_tiles)
        ys = _experts(xs.reshape(-1, d), exp_w1[l], exp_w3[l], exp_w2[l], tile_expert, n_valid)
        xa = _combine(ys.reshape(xs.shape), sel, x1, u2, mod_l, sh_w1[l].astype(BF16), sh_w3[l].astype(BF16),
                      sh_w2[l].astype(BF16), ln2_g[l][None, :], ln2_b[l][None, :], tabs, n_ctx, alpha)
    return xa[:, n_ctx:, :]
```

```python
import functools
import math

import jax
import jax.numpy as jnp
from jax import lax
from jax.experimental import pallas as pl
from jax.experimental.pallas import tpu as pltpu

N_HEADS = 8
N_KV_HEADS = 2
HEAD_DIM = 128
KV_GROUP = N_HEADS // N_KV_HEADS
ATT_BLOCK = 128
GRID_W = 64
ROPE_THETA = 10000.0
RNN_BLOCKS = 16
LRU_C = 8.0
N_EXPERTS = 64
TOP_K = 8
N_GROUPS = 8
TOPK_GROUPS = 4
D_EXPERT = 256
ROUTE_SCALE = 2.5
LN_EPS = 1e-6

ROW_TILE = 256
MXU_TILE = 256
CHUNK = 16
LOCAL_ROWS = TOP_K * ROW_TILE + N_EXPERTS * CHUNK
MAX_CHUNKS = LOCAL_ROWS // CHUNK
EXPERT_ROW_TILE = 512
MOD_ROWS = 24
VMEM_LIMIT = 56 * 1024 * 1024

F32 = jnp.float32
BF16 = jnp.bfloat16
NEG_BIG = -1e30


def _const_spec(shape):
    zeros = (0,) * len(shape)
    return pl.BlockSpec(shape, lambda *_: zeros, pipeline_mode=pl.Buffered(1))


def _params(sem):
    return pltpu.CompilerParams(dimension_semantics=sem, vmem_limit_bytes=VMEM_LIMIT)


def _silu(v):
    return v * jax.nn.sigmoid(v)


def _gelu_tanh(v):
    return v * (0.5 * (1.0 + jnp.tanh(math.sqrt(2.0 / math.pi) * (v + 0.044715 * (v * v * v)))))


def _layer_norm(v, g, b):
    mu = jnp.mean(v, axis=-1, keepdims=True)
    d = v - mu
    var = jnp.mean(d * d, axis=-1, keepdims=True)
    return d * lax.rsqrt(var + LN_EPS) * g + b


def _mod_kernel(cc_ref, w_ref, b_ref, o_ref):
    s = _silu(cc_ref[...]).astype(BF16)
    o_ref[0] = jnp.dot(s, w_ref[0].astype(BF16), preferred_element_type=F32) + b_ref[0]


def _modulation(cc, w_mod, b_mod):
    n_layers, d, d6 = w_mod.shape
    col = 1536
    return pl.pallas_call(
        _mod_kernel,
        out_shape=jax.ShapeDtypeStruct((n_layers, MOD_ROWS, d6), F32),
        grid=(n_layers, d6 // col),
        in_specs=[
            pl.BlockSpec((MOD_ROWS, d), lambda l, j: (0, 0)),
            pl.BlockSpec((1, d, col), lambda l, j: (l, 0, j)),
            pl.BlockSpec((1, 1, col), lambda l, j: (l, 0, j)),
        ],
        out_specs=pl.BlockSpec((1, MOD_ROWS, col), lambda l, j: (l, 0, j)),
        compiler_params=_params(("parallel", "parallel")),
        name="modulation",
    )(cc, w_mod, b_mod.reshape(n_layers, 1, d6))


def _mod_row(mod_ref, n_batch, n_ctx_tiles):
    b = pl.program_id(0)
    j = pl.program_id(1)
    row = jnp.where(j < n_ctx_tiles, n_batch, b)
    return mod_ref[pl.ds(row, 1), :]


def _rope(v, cos_w, sin_w):
    width = v.shape[1]
    lane = lax.broadcasted_iota(jnp.int32, v.shape, 1)
    partner = jnp.where((lane & 32) == 0, pltpu.roll(v, width - 32, 1), pltpu.roll(v, 32, 1))
    return v * cos_w + partner * sin_w


def _inproj_kernel(x_ref, mod_ref, w_ref, cos_ref, sin_ref,
                   q_ref, k_ref, v_ref, xr_ref, gy_ref, sa_ref, sr_ref,
                   *, n_batch, n_ctx_tiles, d):
    m = _mod_row(mod_ref, n_batch, n_ctx_tiles)
    sh1 = m[:, 0:d]
    sc1 = m[:, d:2 * d]
    u = (x_ref[0] * (1.0 + sc1) + sh1).astype(BF16)

    def proj(lo, hi):
        return jnp.dot(u, w_ref[:, lo:hi], preferred_element_type=F32)

    qw = N_HEADS * HEAD_DIM
    kw = N_KV_HEADS * HEAD_DIM
    cos1 = cos_ref[...]
    sin1 = sin_ref[...]
    o = 0
    q = proj(o, o + qw)
    q_ref[0] = _rope(q, jnp.concatenate([cos1] * N_HEADS, axis=1),
                     jnp.concatenate([sin1] * N_HEADS, axis=1)).astype(BF16)
    o += qw
    k = proj(o, o + kw)
    k_ref[0] = _rope(k, jnp.concatenate([cos1] * N_KV_HEADS, axis=1),
                     jnp.concatenate([sin1] * N_KV_HEADS, axis=1)).astype(BF16)
    o += kw
    v_ref[0] = proj(o, o + kw).astype(BF16)
    o += kw
    xr_ref[0] = proj(o, o + d)
    o += d
    gy_ref[0] = _gelu_tanh(proj(o, o + d)).astype(BF16)
    o += d
    sa_ref[0] = jax.nn.sigmoid(proj(o, o + d)).astype(BF16)
    o += d
    sr_ref[0] = jax.nn.sigmoid(proj(o, o + d)).astype(BF16)


def _in_projection(xa, mod_l, w_in, cos_t, sin_t, n_ctx):
    n_batch, t, d = xa.shape
    qw = N_HEADS * HEAD_DIM
    kw = N_KV_HEADS * HEAD_DIM
    row = lambda w: pl.BlockSpec((1, ROW_TILE, w), lambda b, j: (b, j, 0))
    tab = pl.BlockSpec((ROW_TILE, HEAD_DIM), lambda b, j: (j, 0))
    shp = lambda w, dt: jax.ShapeDtypeStruct((n_batch, t, w), dt)
    return pl.pallas_call(
        functools.partial(_inproj_kernel, n_batch=n_batch, n_ctx_tiles=n_ctx // ROW_TILE, d=d),
        out_shape=(shp(qw, BF16), shp(kw, BF16), shp(kw, BF16), shp(d, F32),
                   shp(d, BF16), shp(d, BF16), shp(d, BF16)),
        grid=(n_batch, t // ROW_TILE),
        in_specs=[row(d), _const_spec(mod_l.shape), _const_spec(w_in.shape), tab, tab],
        out_specs=(row(qw), row(kw), row(kw), row(d), row(d), row(d), row(d)),
        compiler_params=_params(("parallel", "parallel")),
        name="in_projection",
    )(xa, mod_l, w_in, cos_t, sin_t)


def _attn_kernel(sink_ref, q_ref, kp_ref, kc_ref, kn_ref, kx_ref, vp_ref, vc_ref, vn_ref, vx_ref,
                 o_ref, *, n_ctx_blocks, n_lat_blocks):
    i = pl.program_id(1) - n_ctx_blocks
    n_ctx = kx_ref.shape[1]
    n_win = 3 * ATT_BLOCK
    n_keys = n_win + n_ctx
    rows = KV_GROUP * ATT_BLOCK
    rr = lax.broadcasted_iota(jnp.int32, (rows, n_keys), 0) & (ATT_BLOCK - 1)
    jj = lax.broadcasted_iota(jnp.int32, (rows, n_keys), 1)
    in_band = (jj >= rr) & (jj <= rr + 2 * ATT_BLOCK)
    left_ok = (i > 0) | (jj >= ATT_BLOCK)
    right_ok = (i < n_lat_blocks - 1) | (jj < 2 * ATT_BLOCK)
    allowed = (jj >= n_win) | (in_band & left_ok & right_ok & (i >= 0))
    rblk = lax.broadcasted_iota(jnp.int32, (rows, 1), 0) // ATT_BLOCK
    scale = HEAD_DIM ** -0.5
    q = q_ref[0]
    for h in range(N_KV_HEADS):
        hs = slice(h * HEAD_DIM, (h + 1) * HEAD_DIM)
        kk = jnp.concatenate([kp_ref[0, :, hs], kc_ref[0, :, hs], kn_ref[0, :, hs], kx_ref[0, :, hs]], axis=0)
        vv = jnp.concatenate([vp_ref[0, :, hs], vc_ref[0, :, hs], vn_ref[0, :, hs], vx_ref[0, :, hs]], axis=0)
        heads = [h * KV_GROUP + g for g in range(KV_GROUP)]
        qs = jnp.concatenate([q[:, hd * HEAD_DIM:(hd + 1) * HEAD_DIM] for hd in heads], axis=0)
        s = lax.dot_general(qs, kk, (((1,), (1,)), ((), ())), preferred_element_type=F32) * scale
        s = jnp.where(allowed, s, NEG_BIG)
        sink = jnp.zeros((rows, 1), F32)
        for g, hd in enumerate(heads):
            sink = jnp.where(rblk == g, sink_ref[hd], sink)
        mx = jnp.maximum(jnp.max(s, axis=1, keepdims=True), sink)
        p = jnp.exp(s - mx)
        den = jnp.sum(p, axis=1, keepdims=True) + jnp.exp(sink - mx)
        o = jnp.dot(p.astype(BF16), vv, preferred_element_type=F32) / den
        for g, hd in enumerate(heads):
            o_ref[0, :, hd * HEAD_DIM:(hd + 1) * HEAD_DIM] = o[g * ATT_BLOCK:(g + 1) * ATT_BLOCK].astype(BF16)


def _attention(sink_l, q, k, v, n_ctx):
    n_batch, t, qw = q.shape
    kw = k.shape[2]
    ncb = n_ctx // ATT_BLOCK
    nlb = (t - n_ctx) // ATT_BLOCK

    def lat_blk(off):
        def index(b, j, sink):
            return (b, ncb + jnp.clip(j - ncb + off, 0, nlb - 1), 0)
        return pl.BlockSpec((1, ATT_BLOCK, kw), index)

    ctx_blk = pl.BlockSpec((1, n_ctx, kw), lambda b, j, sink: (b, 0, 0))
    q_blk = pl.BlockSpec((1, ATT_BLOCK, qw), lambda b, j, sink: (b, j, 0))
    kv_specs = [lat_blk(-1), lat_blk(0), lat_blk(1), ctx_blk]
    return pl.pallas_call(
        functools.partial(_attn_kernel, n_ctx_blocks=ncb, n_lat_blocks=nlb),
        out_shape=jax.ShapeDtypeStruct((n_batch, t, qw), BF16),
        grid_spec=pltpu.PrefetchScalarGridSpec(
            num_scalar_prefetch=1,
            grid=(n_batch, t // ATT_BLOCK),
            in_specs=[q_blk] + kv_specs + kv_specs,
            out_specs=q_blk,
        ),
        compiler_params=_params(("parallel", "parallel")),
        name="attention",
    )(sink_l, q, k, k, k, k, v, v, v, v)


def _rnn_kernel(x_ref, cw_ref, cb_ref, wg_ref, ba_ref, bi_ref, lam_ref, hf_ref, hb_ref,
                a_scr, b_scr, h_scr, *, n_ctx, n_rows):
    j = pl.program_id(1)
    tb = ROW_TILE
    n_blk = n_rows // tb
    n_cblk = n_ctx // tb
    blk_f = j
    blk_b = jnp.where(j < n_cblk, n_cblk - 1 - j, n_blk - 1 - (j - n_cblk))
    d = x_ref.shape[2]
    ridx = lax.broadcasted_iota(jnp.int32, (tb, d), 0)

    def conv_block(blk):
        t0 = pl.multiple_of(blk * tb, tb)
        cur = x_ref[0, pl.ds(t0, tb), :]
        prev8 = x_ref[0, pl.ds(pl.multiple_of(jnp.maximum(t0 - 8, 0), 8), 8), :]
        next8 = x_ref[0, pl.ds(pl.multiple_of(jnp.minimum(t0 + tb, n_rows - 8), 8), 8), :]
        prev_ok = (t0 != 0) & (t0 != n_ctx)
        next_ok = (t0 + tb != n_ctx) & (t0 + tb != n_rows)
        prev8 = jnp.where(prev_ok, prev8, 0.0)
        next8 = jnp.where(next_ok, next8, 0.0)
        p6 = jnp.broadcast_to(prev8[6:7, :], (tb, d))
        p7 = jnp.broadcast_to(prev8[7:8, :], (tb, d))
        n0 = jnp.broadcast_to(next8[0:1, :], (tb, d))
        xm1 = jnp.where(ridx == 0, p7, pltpu.roll(cur, 1, 0))
        xm2 = jnp.where(ridx == 0, p6, jnp.where(ridx == 1, p7, pltpu.roll(cur, 2, 0)))
        xp1 = jnp.where(ridx == tb - 1, n0, pltpu.roll(cur, tb - 1, 0))
        return (cb_ref[...] + xm2 * cw_ref[0:1, :] + xm1 * cw_ref[1:2, :]
                + cur * cw_ref[2:3, :] + xp1 * cw_ref[3:4, :])

    def gate_block(xc, dr):
        xb = xc.astype(BF16)
        r_parts, i_parts = [], []
        for jt in range(d // MXU_TILE):
            g = jnp.dot(xb[:, jt * MXU_TILE:(jt + 1) * MXU_TILE], wg_ref[dr, jt],
                        preferred_element_type=F32)
            r_parts.append(g[:, :MXU_TILE])
            i_parts.append(g[:, MXU_TILE:])
        r = jax.nn.sigmoid(jnp.concatenate(r_parts, axis=1) + ba_ref[dr:dr + 1, :])
        ig = jax.nn.sigmoid(jnp.concatenate(i_parts, axis=1) + bi_ref[dr:dr + 1, :])
        nl = -lam_ref[dr:dr + 1, :]
        softplus = jnp.maximum(nl, 0.0) + jnp.log1p(jnp.exp(-jnp.abs(nl)))
        log_a = (-LRU_C) * r * softplus
        a = jnp.exp(log_a)
        a_scr[dr] = a
        gap = 1.0 - a * a
        root = jnp.where(gap > 0.0, gap * lax.rsqrt(gap), 0.0)
        b_scr[dr] = root * (ig * xc)

    gate_block(conv_block(blk_f), 0)
    gate_block(conv_block(blk_b), 1)

    @pl.when(j == 0)
    def _():
        h_scr[...] = jnp.zeros_like(h_scr)

    def step(s, carry):
        hf, hb = carry
        tf = s
        tr = tb - 1 - s
        hf = a_scr[0, pl.ds(tf, 1), :] * hf + b_scr[0, pl.ds(tf, 1), :]
        hb = a_scr[1, pl.ds(tr, 1), :] * hb + b_scr[1, pl.ds(tr, 1), :]
        hf_ref[0, pl.ds(tf, 1), :] = hf
        hb_ref[0, pl.ds(tr, 1), :] = hb
        return hf, hb

    hf, hb = lax.fori_loop(0, tb, step, (h_scr[0:1, :], h_scr[1:2, :]), unroll=8)
    h_scr[0:1, :] = hf
    h_scr[1:2, :] = hb


def _rnn_branch(xr, conv_w, conv_b, wg, ba, bi, lam, n_ctx):
    n_batch, t, d = xr.shape
    n_blk = t // ROW_TILE
    n_cblk = n_ctx // ROW_TILE

    def bwd_index(b, j):
        return (b, jnp.where(j < n_cblk, n_cblk - 1 - j, n_blk - 1 - (j - n_cblk)), 0)

    out = jax.ShapeDtypeStruct((n_batch, t, d), F32)
    return pl.pallas_call(
        functools.partial(_rnn_kernel, n_ctx=n_ctx, n_rows=t),
        out_shape=(out, out),
        grid=(n_batch, n_blk),
        in_specs=[
            pl.BlockSpec((1, t, d), lambda b, j: (b, 0, 0)),
            _const_spec(conv_w.shape), _const_spec(conv_b.shape), _const_spec(wg.shape),
            _const_spec(ba.shape), _const_spec(bi.shape), _const_spec(lam.shape),
        ],
        out_specs=(pl.BlockSpec((1, ROW_TILE, d), lambda b, j: (b, j, 0)),
                   pl.BlockSpec((1, ROW_TILE, d), bwd_index)),
        scratch_shapes=[pltpu.VMEM((2, ROW_TILE, d), F32), pltpu.VMEM((2, ROW_TILE, d), F32),
                        pltpu.VMEM((8, d), F32)],
        compiler_params=_params(("parallel", "arbitrary")),
        name="rnn_branch",
    )(xr, conv_w, conv_b, wg, ba, bi, lam)


def _dot_nt(a, b):
    return lax.dot_general(a, b, (((1,), (1,)), ((), ())), preferred_element_type=F32)


def _outproj_kernel(x_ref, attn_ref, hf_ref, hb_ref, gy_ref, sa_ref, sr_ref, mod_ref,
                    woa_ref, wor_ref, wout_ref, g_ref, b_ref, rwh_ref, rwl_ref,
                    x1_ref, u2_ref, lg_ref, *, n_batch, n_ctx_tiles, d, alpha):
    m = _mod_row(mod_ref, n_batch, n_ctx_tiles)
    g1 = m[:, 2 * d:3 * d]
    sh2 = m[:, 3 * d:4 * d]
    sc2 = m[:, 4 * d:5 * d]
    a = jnp.dot(attn_ref[0], woa_ref[...], preferred_element_type=F32)
    rnn = ((hf_ref[0] + hb_ref[0]) * gy_ref[0].astype(F32)).astype(BF16)
    r = jnp.dot(rnn, wor_ref[...], preferred_element_type=F32)
    merged = (sa_ref[0].astype(F32) * a + sr_ref[0].astype(F32) * r).astype(BF16)
    mix = jnp.dot(merged, wout_ref[...], preferred_element_type=F32)
    x1 = _layer_norm(alpha * x_ref[0] + g1 * mix, g_ref[...], b_ref[...])
    x1_ref[0] = x1
    u2 = x1 * (1.0 + sc2) + sh2
    u_hi = u2.astype(BF16)
    u2_ref[0] = u_hi
    u_lo = (u2 - u_hi.astype(F32)).astype(BF16)
    lg_ref[...] = (_dot_nt(rwh_ref[...], u_hi) + _dot_nt(rwh_ref[...], u_lo)
                   + _dot_nt(rwl_ref[...], u_hi))


def _out_projection(xa, attn, hf, hb, gy, sa, sr, mod_l, woa, wor, wout, ln_g, ln_b, rwh, rwl,
                    n_ctx, alpha):
    n_batch, t, d = xa.shape
    tiles = t // ROW_TILE
    row = pl.BlockSpec((1, ROW_TILE, d), lambda b, j: (b, j, 0))
    return pl.pallas_call(
        functools.partial(_outproj_kernel, n_batch=n_batch, n_ctx_tiles=n_ctx // ROW_TILE, d=d,
                          alpha=alpha),
        out_shape=(jax.ShapeDtypeStruct((n_batch, t, d), F32),
                   jax.ShapeDtypeStruct((n_batch, t, d), BF16),
                   jax.ShapeDtypeStruct((N_EXPERTS, n_batch * t), F32)),
        grid=(n_batch, tiles),
        in_specs=[row] * 7 + [_const_spec(a.shape) for a in
                              (mod_l, woa, wor, wout, ln_g, ln_b, rwh, rwl)],
        out_specs=(row, row,
                   pl.BlockSpec((N_EXPERTS, ROW_TILE), lambda b, j: (0, b * tiles + j))),
        compiler_params=_params(("parallel", "parallel")),
        name="out_projection",
    )(xa, attn, hf, hb, gy, sa, sr, mod_l, woa, wor, wout, ln_g, ln_b, rwh, rwl)


def _route_kernel(lg_ref, bias_ref, pos_ref, wts_ref, nch_ref, lo_ref):
    lg = lg_ref[...]
    n_tok = lg.shape[1]
    per_group = N_EXPERTS // N_GROUPS
    scores = jax.nn.sigmoid(lg)
    biased = scores + bias_ref[...]
    b3 = biased.reshape(N_GROUPS, per_group, n_tok)
    sub = lax.broadcasted_iota(jnp.int32, b3.shape, 1)
    m1 = jnp.max(b3, axis=1, keepdims=True)
    i1 = jnp.min(jnp.where(b3 == m1, sub, per_group), axis=1, keepdims=True)
    m2 = jnp.max(jnp.where(sub == i1, -jnp.inf, b3), axis=1, keepdims=True)
    grp = (m1 + m2).reshape(N_GROUPS, n_tok)

    gi = lax.broadcasted_iota(jnp.int32, grp.shape, 0)
    gsel = jnp.zeros(grp.shape, F32)
    for _ in range(TOPK_GROUPS):
        m = jnp.max(grp, axis=0, keepdims=True)
        idx = jnp.min(jnp.where(grp == m, gi, N_GROUPS), axis=0, keepdims=True)
        hit = gi == idx
        gsel = jnp.where(hit, 1.0, gsel)
        grp = jnp.where(hit, -jnp.inf, grp)
    emask = jnp.broadcast_to(gsel.reshape(N_GROUPS, 1, n_tok), b3.shape).reshape(N_EXPERTS, n_tok)
    cand = jnp.where(emask > 0.0, biased, -jnp.inf)

    ei = lax.broadcasted_iota(jnp.int32, cand.shape, 0)
    comb = jnp.zeros(cand.shape, F32)
    picked = jnp.zeros(cand.shape, F32)
    hits = []
    for _ in range(TOP_K):
        m = jnp.max(cand, axis=0, keepdims=True)
        idx = jnp.min(jnp.where(cand == m, ei, N_EXPERTS), axis=0, keepdims=True)
        hit = ei == idx
        hits.append(hit)
        comb = jnp.where(hit, scores, comb)
        picked = jnp.where(hit, 1.0, picked)
        cand = jnp.where(hit, -jnp.inf, cand)
    comb = comb / jnp.sum(comb, axis=0, keepdims=True) * ROUTE_SCALE

    cnt = jnp.sum(picked, axis=1, keepdims=True)
    nch = jnp.broadcast_to(jnp.floor((cnt + (CHUNK - 1)) * (1.0 / CHUNK)), (N_EXPERTS, 128))
    erow = lax.broadcasted_iota(jnp.int32, nch.shape, 0)
    incl = nch
    for s in (1, 2, 4, 8, 16, 32):
        incl = incl + jnp.where(erow >= s, pltpu.roll(incl, s, 0), 0.0)
    lo = incl - nch
    earlier = (lax.broadcasted_iota(jnp.int32, (n_tok, n_tok), 0)
               < lax.broadcasted_iota(jnp.int32, (n_tok, n_tok), 1)).astype(BF16)
    rank = jnp.dot(picked.astype(BF16), earlier, preferred_element_type=F32)
    slot = lo[:, 0:1] * CHUNK + rank
    pos = jnp.concatenate([jnp.sum(jnp.where(h, slot, 0.0), axis=0, keepdims=True) for h in hits], axis=0)
    wts = jnp.concatenate([jnp.sum(jnp.where(h, comb, 0.0), axis=0, keepdims=True) for h in hits], axis=0)
    pos_ref[...] = pos.astype(jnp.int32)
    wts_ref[...] = wts
    nch_ref[0] = nch.astype(jnp.int32)
    lo_ref[0] = lo.astype(jnp.int32)


def _route(logits_t, bias_col):
    n_tok = logits_t.shape[1]
    n_tiles = n_tok // ROW_TILE
    tab = jax.ShapeDtypeStruct((n_tiles, N_EXPERTS, 128), jnp.int32)
    tab_spec = pl.BlockSpec((1, N_EXPERTS, 128), lambda i: (i, 0, 0))
    pair_spec = pl.BlockSpec((TOP_K, ROW_TILE), lambda i: (0, i))
    return pl.pallas_call(
        _route_kernel,
        out_shape=(jax.ShapeDtypeStruct((TOP_K, n_tok), jnp.int32),
                   jax.ShapeDtypeStruct((TOP_K, n_tok), F32), tab, tab),
        grid=(n_tiles,),
        in_specs=[pl.BlockSpec((N_EXPERTS, ROW_TILE), lambda i: (0, i)), _const_spec(bias_col.shape)],
        out_specs=(pair_spec, pair_spec, tab_spec, tab_spec),
        compiler_params=_params(("parallel",)),
        name="route",
    )(logits_t, bias_col)


def _chunk_copy(src_ref, src_chunk, dst_ref, dst_chunk, sem):
    return pltpu.make_async_copy(src_ref.at[src_chunk], dst_ref.at[dst_chunk], sem)


def _flat_tile():
    return pl.program_id(0) * pl.num_programs(1) + pl.program_id(1)


def _wait_slot(local_slot_ref, global_ref, sem, to_global):
    whole = global_ref.at[pl.ds(0, MAX_CHUNKS)]
    if to_global:
        pltpu.make_async_copy(local_slot_ref, whole, sem).wait()
    else:
        pltpu.make_async_copy(whole, local_slot_ref, sem).wait()


def _dispatch_kernel(dst_tab, tail_lo, tail_n,
                     x_ref, pos_ref, wts_ref, xs_hbm, sel_ref, loc, zbuf, sems):
    i = _flat_tile()
    n_tiles = pl.num_programs(0) * pl.num_programs(1)
    slot = i % 2
    pos = pos_ref[...]
    wts = wts_ref[...].astype(BF16)
    x = x_ref[0]
    cpb = ROW_TILE // CHUNK
    rid = lax.broadcasted_iota(jnp.int32, (ROW_TILE, ROW_TILE), 0).astype(F32).astype(BF16)
    for rb in range(LOCAL_ROWS // ROW_TILE):
        rel = pos - rb * ROW_TILE
        rel = jnp.where(rel < 0, -1, jnp.where(rel >= ROW_TILE, -1, rel)).astype(F32).astype(BF16)
        pw = jnp.zeros((ROW_TILE, ROW_TILE), BF16)
        for k in range(TOP_K):
            hit = rid == jnp.broadcast_to(rel[k:k + 1, :], rid.shape)
            pw = jnp.where(hit, jnp.broadcast_to(wts[k:k + 1, :], rid.shape), pw)
        p = jnp.where(pw > 0, jnp.ones_like(pw), jnp.zeros_like(pw))
        loc[slot, rb * cpb:(rb + 1) * cpb] = jnp.dot(
            p, x, preferred_element_type=F32).astype(BF16).reshape(cpb, CHUNK, x.shape[1])
        sel_ref[0, rb] = pw.T
        for c in range(rb * cpb, (rb + 1) * cpb):
            _chunk_copy(loc.at[slot], c, xs_hbm, dst_tab[i * MAX_CHUNKS + c], sems.at[slot]).start()

    @pl.when(i > 0)
    def _():
        _wait_slot(loc.at[1 - slot], xs_hbm, sems.at[1 - slot], to_global=True)

    @pl.when(i == n_tiles - 1)
    def _():
        _wait_slot(loc.at[slot], xs_hbm, sems.at[slot], to_global=True)
        zbuf[...] = jnp.zeros_like(zbuf)

        def per_expert(e, n_started):
            def per_chunk(c, carry):
                _chunk_copy(zbuf, 0, xs_hbm, tail_lo[e] + c, sems.at[slot]).start()
                return carry

            lax.fori_loop(0, tail_n[e], per_chunk, 0)
            return n_started + tail_n[e]

        n_started = lax.fori_loop(0, N_EXPERTS, per_expert, 0)

        def drain(c, carry):
            _chunk_copy(zbuf, 0, xs_hbm, 0, sems.at[slot]).wait()
            return carry

        lax.fori_loop(0, n_started, drain, 0)


def _dispatch(u2, pos, wts, tabs, n_row_tiles):
    n_batch, t, d = u2.shape
    tiles = t // ROW_TILE
    n_blocks = LOCAL_ROWS // ROW_TILE
    pair = pl.BlockSpec((TOP_K, ROW_TILE), lambda b, j, *_: (0, b * tiles + j))
    return pl.pallas_call(
        _dispatch_kernel,
        out_shape=(jax.ShapeDtypeStruct((_buffer_chunks(n_row_tiles), CHUNK, d), BF16),
                   jax.ShapeDtypeStruct((n_batch * tiles, n_blocks, ROW_TILE, ROW_TILE), BF16)),
        grid_spec=pltpu.PrefetchScalarGridSpec(
            num_scalar_prefetch=3,
            grid=(n_batch, tiles),
            in_specs=[pl.BlockSpec((1, ROW_TILE, d), lambda b, j, *_: (b, j, 0)), pair, pair],
            out_specs=(pl.BlockSpec(memory_space=pl.ANY),
                       pl.BlockSpec((1, n_blocks, ROW_TILE, ROW_TILE),
                                    lambda b, j, *_: (b * tiles + j, 0, 0, 0))),
            scratch_shapes=[pltpu.VMEM((2, MAX_CHUNKS, CHUNK, d), BF16), pltpu.VMEM((1, CHUNK, d), BF16),
                            pltpu.SemaphoreType.DMA((2,))],
        ),
        compiler_params=_params(("arbitrary", "arbitrary")),
        name="dispatch",
    )(tabs["dst"], tabs["tail_lo"], tabs["tail_n"], u2, pos, wts)


def _experts_kernel(te_tab, nv_tab, xs_ref, w1_ref, w3_ref, w2_ref, ys_ref, w1b, w3b, w2b):
    @pl.when(pl.program_id(0) < nv_tab[0])
    def _():
        j = pl.program_id(0)

        @pl.when((j == 0) | (te_tab[j] != te_tab[jnp.maximum(j - 1, 0)]))
        def _():
            w1b[...] = w1_ref[0, 0].astype(BF16)
            w3b[...] = w3_ref[0, 0].astype(BF16)
            w2b[...] = w2_ref[0, 0].astype(BF16)

        x = xs_ref[...]
        h1 = jnp.dot(x, w1b[...], preferred_element_type=F32)
        h3 = jnp.dot(x, w3b[...], preferred_element_type=F32)
        h = (_silu(h1) * h3).astype(BF16)
        ys_ref[...] = jnp.dot(h, w2b[...], preferred_element_type=F32).astype(BF16)


def _experts(xs, w1, w3, w2, layer, tile_expert, n_valid, n_row_tiles):
    rows, d = xs.shape
    de = w1.shape[3]
    row = pl.BlockSpec((EXPERT_ROW_TILE, d), lambda j, te, nv: (jnp.minimum(j, nv[0] - 1), 0))
    return pl.pallas_call(
        _experts_kernel,
        out_shape=jax.ShapeDtypeStruct((rows, d), BF16),
        grid_spec=pltpu.PrefetchScalarGridSpec(
            num_scalar_prefetch=2,
            grid=(n_row_tiles,),
            in_specs=[row,
                      pl.BlockSpec((1, 1, d, de), lambda j, te, nv: (layer, te[j], 0, 0)),
                      pl.BlockSpec((1, 1, d, de), lambda j, te, nv: (layer, te[j], 0, 0)),
                      pl.BlockSpec((1, 1, de, d), lambda j, te, nv: (layer, te[j], 0, 0))],
            out_specs=row,
            scratch_shapes=[pltpu.VMEM((d, de), BF16), pltpu.VMEM((d, de), BF16),
                            pltpu.VMEM((de, d), BF16)],
        ),
        compiler_params=_params(("arbitrary",)),
        name="experts",
    )(tile_expert, n_valid, xs, w1, w3, w2)


def _combine_kernel(src_tab,
                    ys_hbm, sel_ref, x1_ref, u2_ref, mod_ref, sw1_ref, sw3_ref, sw2_ref, g_ref, b_ref,
                    o_ref, loc, sems, *, n_batch, n_ctx_tiles, d, alpha):
    i = _flat_tile()
    n_tiles = pl.num_programs(0) * pl.num_programs(1)
    slot = i % 2

    def fetch(tile, sl, chunks):
        for c in chunks:
            _chunk_copy(ys_hbm, src_tab[tile * MAX_CHUNKS + c], loc.at[sl], c, sems.at[sl]).start()

    @pl.when(i == 0)
    def _():
        fetch(0, 0, range(MAX_CHUNKS))

    _wait_slot(loc.at[slot], ys_hbm, sems.at[slot], to_global=False)

    nxt = jnp.where(i + 1 < n_tiles, i + 1, 0)
    cpb = ROW_TILE // CHUNK
    routed = jnp.zeros((ROW_TILE, d), F32)
    for rb in range(LOCAL_ROWS // ROW_TILE):
        rows = loc[slot, rb * cpb:(rb + 1) * cpb].reshape(ROW_TILE, d)
        routed = routed + jnp.dot(sel_ref[0, rb], rows, preferred_element_type=F32)
        fetch(nxt, 1 - slot, range(rb * cpb, (rb + 1) * cpb))

    u = u2_ref[0]
    hs = (_silu(jnp.dot(u, sw1_ref[...], preferred_element_type=F32))
          * jnp.dot(u, sw3_ref[...], preferred_element_type=F32)).astype(BF16)
    shared = jnp.dot(hs, sw2_ref[...], preferred_element_type=F32)

    m = _mod_row(mod_ref, n_batch, n_ctx_tiles)
    g2 = m[:, 5 * d:6 * d]
    o_ref[0] = _layer_norm(alpha * x1_ref[0] + g2 * (routed + shared), g_ref[...], b_ref[...])

    @pl.when(i == n_tiles - 1)
    def _():
        _wait_slot(loc.at[1 - slot], ys_hbm, sems.at[1 - slot], to_global=False)


def _combine(ys, sel, x1, u2, mod_l, sw1, sw3, sw2, ln_g, ln_b, tabs, n_ctx, alpha):
    n_batch, t, d = x1.shape
    tiles = t // ROW_TILE
    n_blocks = LOCAL_ROWS // ROW_TILE
    row = pl.BlockSpec((1, ROW_TILE, d), lambda b, j, *_: (b, j, 0))
    return pl.pallas_call(
        functools.partial(_combine_kernel, n_batch=n_batch, n_ctx_tiles=n_ctx // ROW_TILE, d=d,
                          alpha=alpha),
        out_shape=jax.ShapeDtypeStruct((n_batch, t, d), F32),
        grid_spec=pltpu.PrefetchScalarGridSpec(
            num_scalar_prefetch=1,
            grid=(n_batch, tiles),
            in_specs=[pl.BlockSpec(memory_space=pl.ANY),
                      pl.BlockSpec((1, n_blocks, ROW_TILE, ROW_TILE),
                                   lambda b, j, *_: (b * tiles + j, 0, 0, 0)),
                      row, row] + [_const_spec(a.shape) for a in (mod_l, sw1, sw3, sw2, ln_g, ln_b)],
            out_specs=row,
            scratch_shapes=[pltpu.VMEM((2, MAX_CHUNKS, CHUNK, d), BF16), pltpu.SemaphoreType.DMA((2,))],
        ),
        compiler_params=_params(("arbitrary", "arbitrary")),
        name="combine",
    )(tabs["src"], ys, sel, x1, u2, mod_l, sw1, sw3, sw2, ln_g, ln_b)


def _dispatch_tables(nch, lo, n_row_tiles):
    cpt = EXPERT_ROW_TILE // CHUNK
    tot_e = jnp.sum(nch, axis=0)
    region = (tot_e + cpt - 1) // cpt * cpt
    ends = jnp.cumsum(region)
    base = ends - region
    g = base[None, :] + jnp.cumsum(nch, axis=0) - nch
    c = jnp.arange(MAX_CHUNKS, dtype=nch.dtype)[None, :, None]
    covers = (lo[:, None, :] <= c) & (c < (lo + nch)[:, None, :])
    used = jnp.any(covers, axis=-1)
    dst = c[:, :, 0] + jnp.sum(jnp.where(covers, (g - lo)[:, None, :], 0), axis=-1)
    slot = (jnp.arange(nch.shape[0], dtype=nch.dtype) % 2)[:, None]
    spare = n_row_tiles * cpt + slot * MAX_CHUNKS + c[:, :, 0]
    row_tile = jnp.arange(n_row_tiles, dtype=nch.dtype)[:, None]
    tile_expert = jnp.sum((ends // cpt)[None, :] <= row_tile, axis=-1)
    i32 = lambda a: a.astype(jnp.int32)
    tabs = dict(dst=i32(jnp.where(used, dst, spare).reshape(-1)),
                src=i32(jnp.where(used, dst, 0).reshape(-1)),
                tail_lo=i32(base + tot_e), tail_n=i32(region - tot_e))
    return tabs, i32(jnp.minimum(tile_expert, N_EXPERTS - 1)), i32(ends[-1:] // cpt)


def _buffer_chunks(n_row_tiles):
    return n_row_tiles * (EXPERT_ROW_TILE // CHUNK) + 2 * MAX_CHUNKS


def _rope_tables(n_ctx, n_lat):
    pos = jnp.arange(n_lat)
    n_freq = HEAD_DIM // 4
    inv = ROPE_THETA ** (-jnp.arange(n_freq, dtype=F32) / n_freq)
    ang_r = (pos // GRID_W).astype(F32)[:, None] * inv
    ang_c = (pos % GRID_W).astype(F32)[:, None] * inv
    cos_l = jnp.concatenate([jnp.cos(ang_r)] * 2 + [jnp.cos(ang_c)] * 2, axis=1)
    sin_l = jnp.concatenate([-jnp.sin(ang_r), jnp.sin(ang_r), -jnp.sin(ang_c), jnp.sin(ang_c)], axis=1)
    cos_t = jnp.concatenate([jnp.ones((n_ctx, HEAD_DIM), F32), cos_l], axis=0)
    sin_t = jnp.concatenate([jnp.zeros((n_ctx, HEAD_DIM), F32), sin_l], axis=0)
    return cos_t, sin_t


def _pack_gate_weights(wa, wi):
    n_dir, n_blocks, w, _ = wa.shape
    per_tile = MXU_TILE // w
    n_tiles = n_blocks // per_tile

    def dense(wx):
        wx = wx.reshape(n_dir, n_tiles, per_tile, w, w)
        eye = jnp.eye(per_tile, dtype=wx.dtype)
        full = jnp.einsum('dtpij,pq->dtpiqj', wx, eye)
        return full.reshape(n_dir, n_tiles, MXU_TILE, MXU_TILE)

    return jnp.concatenate([dense(wa), dense(wi)], axis=-1).astype(BF16)


def kernel(x, c, ctx, c_ctx, w_mod, b_mod, w_in, sink, conv_w, conv_b, rg_wa, rg_ba, rg_wi, rg_bi,
           rg_lambda, w_o_attn, w_o_rnn, w_out, ln1_g, ln1_b, router_w, router_bias, exp_w1, exp_w3,
           exp_w2, sh_w1, sh_w3, sh_w2, ln2_g, ln2_b):
    n_batch, n_lat, d = x.shape
    n_ctx = ctx.shape[1]
    n_layers = w_mod.shape[0]
    t = n_ctx + n_lat
    assert n_ctx % ROW_TILE == 0 and n_lat % ROW_TILE == 0 and n_batch + 1 <= MOD_ROWS
    assert exp_w1.shape[1:] == (N_EXPERTS, d, D_EXPERT) and sh_w1.shape[1:] == (d, D_EXPERT)
    alpha = (2 * n_layers) ** 0.25
    n_tiles = n_batch * t // ROW_TILE
    chunks_per_row_tile = EXPERT_ROW_TILE // CHUNK
    max_chunks = (TOP_K * n_batch * t // CHUNK + n_tiles * N_EXPERTS
                  + N_EXPERTS * (chunks_per_row_tile - 1))
    n_row_tiles = pl.cdiv(max_chunks, chunks_per_row_tile)

    xa = jnp.concatenate([ctx, x], axis=1)
    cc = jnp.zeros((MOD_ROWS, d), F32).at[:n_batch].set(c).at[n_batch].set(c_ctx)
    mod = _modulation(cc, w_mod, b_mod)
    cos_t, sin_t = _rope_tables(n_ctx, n_lat)

    for l in range(n_layers):
        mod_l = mod[l]
        q, k, v, xr, gy, sa, sr = _in_projection(xa, mod_l, w_in[l].astype(BF16), cos_t, sin_t, n_ctx)
        attn = _attention(sink[l], q, k, v, n_ctx)
        wg = _pack_gate_weights(rg_wa[l], rg_wi[l])
        hf, hb = _rnn_branch(xr, conv_w[l], conv_b[l][None, :], wg, rg_ba[l], rg_bi[l], rg_lambda[l],
                             n_ctx)
        rw_t = router_w[l].T
        rw_hi = rw_t.astype(BF16)
        rw_lo = (rw_t - rw_hi.astype(F32)).astype(BF16)
        x1, u2, logits_t = _out_projection(
            xa, attn, hf, hb, gy, sa, sr, mod_l, w_o_attn[l].astype(BF16), w_o_rnn[l].astype(BF16),
            w_out[l].astype(BF16), ln1_g[l][None, :], ln1_b[l][None, :], rw_hi, rw_lo, n_ctx, alpha)
        pos, wts, nch, lo = _route(logits_t, router_bias[l][:, None])
        tabs, tile_expert, n_valid = _dispatch_tables(nch[:, :, 0], lo[:, :, 0], n_row_tiles)
        xs, sel = _dispatch(u2, pos, wts, tabs, n_row_tiles)
        ys = _experts(xs.reshape(-1, d), exp_w1, exp_w3, exp_w2, l, tile_expert, n_valid, n_row_tiles)
        xa = _combine(ys.reshape(xs.shape), sel, x1, u2, mod_l, sh_w1[l].astype(BF16), sh_w3[l].astype(BF16),
                      sh_w2[l].astype(BF16), ln2_g[l][None, :], ln2_b[l][None, :], tabs, n_ctx, alpha)
    return xa[:, n_ctx:, :]
```

```python
import functools
import math

import jax
import jax.numpy as jnp
from jax import lax
from jax.experimental import pallas as pl
from jax.experimental.pallas import tpu as pltpu

N_HEADS = 8
N_KV_HEADS = 2
HEAD_DIM = 128
KV_GROUP = N_HEADS // N_KV_HEADS
ATT_BLOCK = 128
GRID_W = 64
ROPE_THETA = 10000.0
RNN_BLOCKS = 16
LRU_C = 8.0
N_EXPERTS = 64
TOP_K = 8
N_GROUPS = 8
TOPK_GROUPS = 4
D_EXPERT = 256
ROUTE_SCALE = 2.5
LN_EPS = 1e-6

ROW_TILE = 256
MXU_TILE = 256
CHUNK = 16
LOCAL_ROWS = TOP_K * ROW_TILE + N_EXPERTS * CHUNK
MAX_CHUNKS = LOCAL_ROWS // CHUNK
EXPERT_ROW_TILE = 1024
COMBINE_SLOTS = 3
MOD_ROWS = 24
VMEM_LIMIT = 56 * 1024 * 1024

F32 = jnp.float32
BF16 = jnp.bfloat16
NEG_BIG = -1e30


def _const_spec(shape):
    zeros = (0,) * len(shape)
    return pl.BlockSpec(shape, lambda *_: zeros, pipeline_mode=pl.Buffered(1))


def _params(sem):
    return pltpu.CompilerParams(dimension_semantics=sem, vmem_limit_bytes=VMEM_LIMIT)


def _silu(v):
    return v * jax.nn.sigmoid(v)


def _gelu_tanh(v):
    return v * (0.5 * (1.0 + jnp.tanh(math.sqrt(2.0 / math.pi) * (v + 0.044715 * (v * v * v)))))


def _layer_norm(v, g, b):
    mu = jnp.mean(v, axis=-1, keepdims=True)
    d = v - mu
    var = jnp.mean(d * d, axis=-1, keepdims=True)
    return d * lax.rsqrt(var + LN_EPS) * g + b


def _mod_kernel(cc_ref, w_ref, b_ref, o_ref):
    s = _silu(cc_ref[...]).astype(BF16)
    o_ref[0] = jnp.dot(s, w_ref[0].astype(BF16), preferred_element_type=F32) + b_ref[0]


def _modulation(cc, w_mod, b_mod):
    n_layers, d, d6 = w_mod.shape
    col = 1536
    return pl.pallas_call(
        _mod_kernel,
        out_shape=jax.ShapeDtypeStruct((n_layers, MOD_ROWS, d6), F32),
        grid=(n_layers, d6 // col),
        in_specs=[
            pl.BlockSpec((MOD_ROWS, d), lambda l, j: (0, 0)),
            pl.BlockSpec((1, d, col), lambda l, j: (l, 0, j)),
            pl.BlockSpec((1, 1, col), lambda l, j: (l, 0, j)),
        ],
        out_specs=pl.BlockSpec((1, MOD_ROWS, col), lambda l, j: (l, 0, j)),
        compiler_params=_params(("parallel", "parallel")),
        name="modulation",
    )(cc, w_mod, b_mod.reshape(n_layers, 1, d6))


def _mod_row(mod_ref, n_batch, n_ctx_tiles):
    b = pl.program_id(0)
    j = pl.program_id(1)
    row = jnp.where(j < n_ctx_tiles, n_batch, b)
    return mod_ref[pl.ds(row, 1), :]


def _rope(v, cos_w, sin_w):
    width = v.shape[1]
    lane = lax.broadcasted_iota(jnp.int32, v.shape, 1)
    partner = jnp.where((lane & 32) == 0, pltpu.roll(v, width - 32, 1), pltpu.roll(v, 32, 1))
    return v * cos_w + partner * sin_w


def _inproj_kernel(x_ref, mod_ref, w_ref, cos_ref, sin_ref,
                   q_ref, k_ref, v_ref, xr_ref, gy_ref, sa_ref, sr_ref,
                   *, n_batch, n_ctx_tiles, d):
    m = _mod_row(mod_ref, n_batch, n_ctx_tiles)
    sh1 = m[:, 0:d]
    sc1 = m[:, d:2 * d]
    u = (x_ref[0] * (1.0 + sc1) + sh1).astype(BF16)

    def proj(lo, hi):
        return jnp.dot(u, w_ref[:, lo:hi], preferred_element_type=F32)

    qw = N_HEADS * HEAD_DIM
    kw = N_KV_HEADS * HEAD_DIM
    cos1 = cos_ref[...]
    sin1 = sin_ref[...]
    o = 0
    q = proj(o, o + qw) * (HEAD_DIM ** -0.5)
    q_ref[0] = _rope(q, jnp.concatenate([cos1] * N_HEADS, axis=1),
                     jnp.concatenate([sin1] * N_HEADS, axis=1)).astype(BF16)
    o += qw
    k = proj(o, o + kw)
    k_ref[0] = _rope(k, jnp.concatenate([cos1] * N_KV_HEADS, axis=1),
                     jnp.concatenate([sin1] * N_KV_HEADS, axis=1)).astype(BF16)
    o += kw
    v_ref[0] = proj(o, o + kw).astype(BF16)
    o += kw
    xr_ref[0] = proj(o, o + d)
    o += d
    gy_ref[0] = _gelu_tanh(proj(o, o + d)).astype(BF16)
    o += d
    sa_ref[0] = jax.nn.sigmoid(proj(o, o + d)).astype(BF16)
    o += d
    sr_ref[0] = jax.nn.sigmoid(proj(o, o + d)).astype(BF16)


def _in_projection(xa, mod_l, w_in, cos_t, sin_t, n_ctx):
    n_batch, t, d = xa.shape
    qw = N_HEADS * HEAD_DIM
    kw = N_KV_HEADS * HEAD_DIM
    row = lambda w: pl.BlockSpec((1, ROW_TILE, w), lambda b, j: (b, j, 0))
    tab = pl.BlockSpec((ROW_TILE, HEAD_DIM), lambda b, j: (j, 0))
    shp = lambda w, dt: jax.ShapeDtypeStruct((n_batch, t, w), dt)
    return pl.pallas_call(
        functools.partial(_inproj_kernel, n_batch=n_batch, n_ctx_tiles=n_ctx // ROW_TILE, d=d),
        out_shape=(shp(qw, BF16), shp(kw, BF16), shp(kw, BF16), shp(d, F32),
                   shp(d, BF16), shp(d, BF16), shp(d, BF16)),
        grid=(n_batch, t // ROW_TILE),
        in_specs=[row(d), _const_spec(mod_l.shape), _const_spec(w_in.shape), tab, tab],
        out_specs=(row(qw), row(kw), row(kw), row(d), row(d), row(d), row(d)),
        compiler_params=_params(("parallel", "parallel")),
        name="in_projection",
    )(xa, mod_l, w_in, cos_t, sin_t)


def _attn_mask_bias(n_ctx):
    n_win = 3 * ATT_BLOCK
    rr = jnp.arange(ATT_BLOCK)[:, None]
    jj = jnp.arange(n_win + n_ctx)[None, :]
    in_band = (jj >= rr) & (jj <= rr + 2 * ATT_BLOCK)
    is_ctx = jj >= n_win
    kinds = [is_ctx | jnp.zeros_like(in_band)]
    for has_left, has_right in ((False, False), (True, False), (False, True), (True, True)):
        ok = in_band & (has_left | (jj >= ATT_BLOCK)) & (has_right | (jj < 2 * ATT_BLOCK))
        kinds.append(is_ctx | ok)
    return jnp.where(jnp.stack(kinds), 0.0, NEG_BIG).astype(F32)


def _attn_kernel(sink_ref, q_ref, kp_ref, kc_ref, kn_ref, kx_ref, vp_ref, vc_ref, vn_ref, vx_ref,
                 bias_ref, o_ref, *, n_ctx_blocks, n_lat_blocks):
    i = pl.program_id(1) - n_ctx_blocks
    kind = jnp.where(i < 0, 0, 1 + (i > 0).astype(jnp.int32) + 2 * (i < n_lat_blocks - 1).astype(jnp.int32))
    rows = KV_GROUP * ATT_BLOCK
    bias1 = bias_ref[kind]
    bias = jnp.concatenate([bias1] * KV_GROUP, axis=0)
    rblk = lax.broadcasted_iota(jnp.int32, (rows, 1), 0) // ATT_BLOCK
    q = q_ref[0]
    for h in range(N_KV_HEADS):
        hs = slice(h * HEAD_DIM, (h + 1) * HEAD_DIM)
        kk = jnp.concatenate([kp_ref[0, :, hs], kc_ref[0, :, hs], kn_ref[0, :, hs], kx_ref[0, :, hs]], axis=0)
        vv = jnp.concatenate([vp_ref[0, :, hs], vc_ref[0, :, hs], vn_ref[0, :, hs], vx_ref[0, :, hs]], axis=0)
        heads = [h * KV_GROUP + g for g in range(KV_GROUP)]
        qs = jnp.concatenate([q[:, hd * HEAD_DIM:(hd + 1) * HEAD_DIM] for hd in heads], axis=0)
        s = lax.dot_general(qs, kk, (((1,), (1,)), ((), ())), preferred_element_type=F32) + bias
        sink = jnp.zeros((rows, 1), F32)
        for g, hd in enumerate(heads):
            sink = jnp.where(rblk == g, sink_ref[hd], sink)
        mx = jnp.maximum(jnp.max(s, axis=1, keepdims=True), sink)
        p = jnp.exp(s - mx)
        den = jnp.sum(p, axis=1, keepdims=True) + jnp.exp(sink - mx)
        o = jnp.dot(p.astype(BF16), vv, preferred_element_type=F32) / den
        for g, hd in enumerate(heads):
            o_ref[0, :, hd * HEAD_DIM:(hd + 1) * HEAD_DIM] = o[g * ATT_BLOCK:(g + 1) * ATT_BLOCK].astype(BF16)


def _attention(sink_l, q, k, v, mask_bias, n_ctx):
    n_batch, t, qw = q.shape
    kw = k.shape[2]
    ncb = n_ctx // ATT_BLOCK
    nlb = (t - n_ctx) // ATT_BLOCK

    def lat_blk(off):
        def index(b, j, sink):
            return (b, ncb + jnp.clip(j - ncb + off, 0, nlb - 1), 0)
        return pl.BlockSpec((1, ATT_BLOCK, kw), index)

    ctx_blk = pl.BlockSpec((1, n_ctx, kw), lambda b, j, sink: (b, 0, 0))
    q_blk = pl.BlockSpec((1, ATT_BLOCK, qw), lambda b, j, sink: (b, j, 0))
    kv_specs = [lat_blk(-1), lat_blk(0), lat_blk(1), ctx_blk]
    return pl.pallas_call(
        functools.partial(_attn_kernel, n_ctx_blocks=ncb, n_lat_blocks=nlb),
        out_shape=jax.ShapeDtypeStruct((n_batch, t, qw), BF16),
        grid_spec=pltpu.PrefetchScalarGridSpec(
            num_scalar_prefetch=1,
            grid=(n_batch, t // ATT_BLOCK),
            in_specs=[q_blk] + kv_specs + kv_specs + [_const_spec(mask_bias.shape)],
            out_specs=q_blk,
        ),
        compiler_params=_params(("parallel", "parallel")),
        name="attention",
    )(sink_l, q, k, k, k, k, v, v, v, v, mask_bias)


def _rnn_kernel(x_ref, cw_ref, cb_ref, wg_ref, ba_ref, bi_ref, lam_ref, hf_ref, hb_ref,
                a_scr, b_scr, h_scr, *, n_ctx, n_rows):
    j = pl.program_id(1)
    tb = ROW_TILE
    n_blk = n_rows // tb
    n_cblk = n_ctx // tb
    blk_f = j
    blk_b = jnp.where(j < n_cblk, n_cblk - 1 - j, n_blk - 1 - (j - n_cblk))
    d = x_ref.shape[2]
    ridx = lax.broadcasted_iota(jnp.int32, (tb, d), 0)

    def conv_block(blk):
        t0 = pl.multiple_of(blk * tb, tb)
        cur = x_ref[0, pl.ds(t0, tb), :]
        prev8 = x_ref[0, pl.ds(pl.multiple_of(jnp.maximum(t0 - 8, 0), 8), 8), :]
        next8 = x_ref[0, pl.ds(pl.multiple_of(jnp.minimum(t0 + tb, n_rows - 8), 8), 8), :]
        prev_ok = (t0 != 0) & (t0 != n_ctx)
        next_ok = (t0 + tb != n_ctx) & (t0 + tb != n_rows)
        prev8 = jnp.where(prev_ok, prev8, 0.0)
        next8 = jnp.where(next_ok, next8, 0.0)
        p6 = jnp.broadcast_to(prev8[6:7, :], (tb, d))
        p7 = jnp.broadcast_to(prev8[7:8, :], (tb, d))
        n0 = jnp.broadcast_to(next8[0:1, :], (tb, d))
        xm1 = jnp.where(ridx == 0, p7, pltpu.roll(cur, 1, 0))
        xm2 = jnp.where(ridx == 0, p6, jnp.where(ridx == 1, p7, pltpu.roll(cur, 2, 0)))
        xp1 = jnp.where(ridx == tb - 1, n0, pltpu.roll(cur, tb - 1, 0))
        return (cb_ref[...] + xm2 * cw_ref[0:1, :] + xm1 * cw_ref[1:2, :]
                + cur * cw_ref[2:3, :] + xp1 * cw_ref[3:4, :])

    def gate_block(xc, dr):
        xb = xc.astype(BF16)
        r_parts, i_parts = [], []
        for jt in range(d // MXU_TILE):
            g = jnp.dot(xb[:, jt * MXU_TILE:(jt + 1) * MXU_TILE], wg_ref[dr, jt],
                        preferred_element_type=F32)
            r_parts.append(g[:, :MXU_TILE])
            i_parts.append(g[:, MXU_TILE:])
        r = jax.nn.sigmoid(jnp.concatenate(r_parts, axis=1) + ba_ref[dr:dr + 1, :])
        ig = jax.nn.sigmoid(jnp.concatenate(i_parts, axis=1) + bi_ref[dr:dr + 1, :])
        nl = -lam_ref[dr:dr + 1, :]
        softplus = jnp.maximum(nl, 0.0) + jnp.log1p(jnp.exp(-jnp.abs(nl)))
        log_a = (-LRU_C) * r * softplus
        a = jnp.exp(log_a)
        a_scr[dr] = a
        gap = 1.0 - a * a
        root = jnp.where(gap > 0.0, gap * lax.rsqrt(gap), 0.0)
        b_scr[dr] = root * (ig * xc)

    gate_block(conv_block(blk_f), 0)
    gate_block(conv_block(blk_b), 1)

    @pl.when(j == 0)
    def _():
        h_scr[...] = jnp.zeros_like(h_scr)

    def step(s, carry):
        hf, hb = carry
        tf = s
        tr = tb - 1 - s
        hf = a_scr[0, pl.ds(tf, 1), :] * hf + b_scr[0, pl.ds(tf, 1), :]
        hb = a_scr[1, pl.ds(tr, 1), :] * hb + b_scr[1, pl.ds(tr, 1), :]
        hf_ref[0, pl.ds(tf, 1), :] = hf
        hb_ref[0, pl.ds(tr, 1), :] = hb
        return hf, hb

    hf, hb = lax.fori_loop(0, tb, step, (h_scr[0:1, :], h_scr[1:2, :]), unroll=8)
    h_scr[0:1, :] = hf
    h_scr[1:2, :] = hb


def _rnn_branch(xr, conv_w, conv_b, wg, ba, bi, lam, n_ctx):
    n_batch, t, d = xr.shape
    n_blk = t // ROW_TILE
    n_cblk = n_ctx // ROW_TILE

    def bwd_index(b, j):
        return (b, jnp.where(j < n_cblk, n_cblk - 1 - j, n_blk - 1 - (j - n_cblk)), 0)

    out = jax.ShapeDtypeStruct((n_batch, t, d), F32)
    return pl.pallas_call(
        functools.partial(_rnn_kernel, n_ctx=n_ctx, n_rows=t),
        out_shape=(out, out),
        grid=(n_batch, n_blk),
        in_specs=[
            pl.BlockSpec((1, t, d), lambda b, j: (b, 0, 0)),
            _const_spec(conv_w.shape), _const_spec(conv_b.shape), _const_spec(wg.shape),
            _const_spec(ba.shape), _const_spec(bi.shape), _const_spec(lam.shape),
        ],
        out_specs=(pl.BlockSpec((1, ROW_TILE, d), lambda b, j: (b, j, 0)),
                   pl.BlockSpec((1, ROW_TILE, d), bwd_index)),
        scratch_shapes=[pltpu.VMEM((2, ROW_TILE, d), F32), pltpu.VMEM((2, ROW_TILE, d), F32),
                        pltpu.VMEM((8, d), F32)],
        compiler_params=_params(("parallel", "arbitrary")),
        name="rnn_branch",
    )(xr, conv_w, conv_b, wg, ba, bi, lam)


def _dot_nt(a, b):
    return lax.dot_general(a, b, (((1,), (1,)), ((), ())), preferred_element_type=F32)


def _outproj_kernel(x_ref, attn_ref, hf_ref, hb_ref, gy_ref, sa_ref, sr_ref, mod_ref,
                    woa_ref, wor_ref, wout_ref, g_ref, b_ref, rwh_ref, rwl_ref,
                    x1_ref, u2_ref, lg_ref, *, n_batch, n_ctx_tiles, d, alpha, sub):
    b = pl.program_id(0)
    j = pl.program_id(1)
    for s in range(sub):
        rs = slice(s * ROW_TILE, (s + 1) * ROW_TILE)
        mrow = jnp.where(j * sub + s < n_ctx_tiles, n_batch, b)
        m = mod_ref[pl.ds(mrow, 1), :]
        g1 = m[:, 2 * d:3 * d]
        sh2 = m[:, 3 * d:4 * d]
        sc2 = m[:, 4 * d:5 * d]
        a = jnp.dot(attn_ref[0, rs, :], woa_ref[...], preferred_element_type=F32)
        rnn = ((hf_ref[0, rs, :] + hb_ref[0, rs, :]) * gy_ref[0, rs, :].astype(F32)).astype(BF16)
        r = jnp.dot(rnn, wor_ref[...], preferred_element_type=F32)
        merged = (sa_ref[0, rs, :].astype(F32) * a + sr_ref[0, rs, :].astype(F32) * r).astype(BF16)
        mix = jnp.dot(merged, wout_ref[...], preferred_element_type=F32)
        x1 = _layer_norm(alpha * x_ref[0, rs, :] + g1 * mix, g_ref[...], b_ref[...])
        x1_ref[0, rs, :] = x1
        u2 = x1 * (1.0 + sc2) + sh2
        u_hi = u2.astype(BF16)
        u2_ref[0, rs, :] = u_hi
        u_lo = (u2 - u_hi.astype(F32)).astype(BF16)
        lg_ref[:, rs] = (_dot_nt(rwh_ref[...], u_hi) + _dot_nt(rwh_ref[...], u_lo)
                         + _dot_nt(rwl_ref[...], u_hi))


def _out_projection(xa, attn, hf, hb, gy, sa, sr, mod_l, woa, wor, wout, ln_g, ln_b, rwh, rwl,
                    n_ctx, alpha):
    n_batch, t, d = xa.shape
    tiles = t // ROW_TILE
    sub = next(s for s in (3, 2, 1) if tiles % s == 0)
    steps = tiles // sub
    row = pl.BlockSpec((1, sub * ROW_TILE, d), lambda b, j: (b, j, 0))
    return pl.pallas_call(
        functools.partial(_outproj_kernel, n_batch=n_batch, n_ctx_tiles=n_ctx // ROW_TILE, d=d,
                          alpha=alpha, sub=sub),
        out_shape=(jax.ShapeDtypeStruct((n_batch, t, d), F32),
                   jax.ShapeDtypeStruct((n_batch, t, d), BF16),
                   jax.ShapeDtypeStruct((N_EXPERTS, n_batch * t), F32)),
        grid=(n_batch, steps),
        in_specs=[row] * 7 + [_const_spec(a.shape) for a in
                              (mod_l, woa, wor, wout, ln_g, ln_b, rwh, rwl)],
        out_specs=(row, row,
                   pl.BlockSpec((N_EXPERTS, sub * ROW_TILE), lambda b, j: (0, b * steps + j))),
        compiler_params=_params(("parallel", "parallel")),
        name="out_projection",
    )(xa, attn, hf, hb, gy, sa, sr, mod_l, woa, wor, wout, ln_g, ln_b, rwh, rwl)


def _route_kernel(lg_ref, bias_ref, pos_ref, wts_ref, nch_ref, lo_ref):
    lg = lg_ref[...]
    n_tok = lg.shape[1]
    per_group = N_EXPERTS // N_GROUPS
    scores = jax.nn.sigmoid(lg)
    biased = scores + bias_ref[...]
    b3 = biased.reshape(N_GROUPS, per_group, n_tok)
    sub = lax.broadcasted_iota(jnp.int32, b3.shape, 1)
    m1 = jnp.max(b3, axis=1, keepdims=True)
    i1 = jnp.min(jnp.where(b3 == m1, sub, per_group), axis=1, keepdims=True)
    m2 = jnp.max(jnp.where(sub == i1, -jnp.inf, b3), axis=1, keepdims=True)
    grp = (m1 + m2).reshape(N_GROUPS, n_tok)

    gi = lax.broadcasted_iota(jnp.int32, grp.shape, 0)
    gsel = jnp.zeros(grp.shape, F32)
    for _ in range(TOPK_GROUPS):
        m = jnp.max(grp, axis=0, keepdims=True)
        idx = jnp.min(jnp.where(grp == m, gi, N_GROUPS), axis=0, keepdims=True)
        hit = gi == idx
        gsel = jnp.where(hit, 1.0, gsel)
        grp = jnp.where(hit, -jnp.inf, grp)
    emask = jnp.broadcast_to(gsel.reshape(N_GROUPS, 1, n_tok), b3.shape).reshape(N_EXPERTS, n_tok)
    cand = jnp.where(emask > 0.0, biased, -jnp.inf)

    ei = lax.broadcasted_iota(jnp.int32, cand.shape, 0)
    comb = jnp.zeros(cand.shape, F32)
    picked = jnp.zeros(cand.shape, F32)
    hits = []
    for _ in range(TOP_K):
        m = jnp.max(cand, axis=0, keepdims=True)
        idx = jnp.min(jnp.where(cand == m, ei, N_EXPERTS), axis=0, keepdims=True)
        hit = ei == idx
        hits.append(hit)
        comb = jnp.where(hit, scores, comb)
        picked = jnp.where(hit, 1.0, picked)
        cand = jnp.where(hit, -jnp.inf, cand)
    comb = comb / jnp.sum(comb, axis=0, keepdims=True) * ROUTE_SCALE

    cnt = jnp.sum(picked, axis=1, keepdims=True)
    nch = jnp.broadcast_to(jnp.floor((cnt + (CHUNK - 1)) * (1.0 / CHUNK)), (N_EXPERTS, 128))
    erow = lax.broadcasted_iota(jnp.int32, nch.shape, 0)
    incl = nch
    for s in (1, 2, 4, 8, 16, 32):
        incl = incl + jnp.where(erow >= s, pltpu.roll(incl, s, 0), 0.0)
    lo = incl - nch
    earlier = (lax.broadcasted_iota(jnp.int32, (n_tok, n_tok), 0)
               < lax.broadcasted_iota(jnp.int32, (n_tok, n_tok), 1)).astype(BF16)
    rank = jnp.dot(picked.astype(BF16), earlier, preferred_element_type=F32)
    slot = lo[:, 0:1] * CHUNK + rank
    pos = jnp.concatenate([jnp.sum(jnp.where(h, slot, 0.0), axis=0, keepdims=True) for h in hits], axis=0)
    wts = jnp.concatenate([jnp.sum(jnp.where(h, comb, 0.0), axis=0, keepdims=True) for h in hits], axis=0)
    pos_ref[...] = pos.astype(jnp.int32)
    wts_ref[...] = wts
    nch_ref[0] = nch.astype(jnp.int32)
    lo_ref[0] = lo.astype(jnp.int32)


def _route(logits_t, bias_col):
    n_tok = logits_t.shape[1]
    n_tiles = n_tok // ROW_TILE
    tab = jax.ShapeDtypeStruct((n_tiles, N_EXPERTS, 128), jnp.int32)
    tab_spec = pl.BlockSpec((1, N_EXPERTS, 128), lambda i: (i, 0, 0))
    pair_spec = pl.BlockSpec((TOP_K, ROW_TILE), lambda i: (0, i))
    return pl.pallas_call(
        _route_kernel,
        out_shape=(jax.ShapeDtypeStruct((TOP_K, n_tok), jnp.int32),
                   jax.ShapeDtypeStruct((TOP_K, n_tok), F32), tab, tab),
        grid=(n_tiles,),
        in_specs=[pl.BlockSpec((N_EXPERTS, ROW_TILE), lambda i: (0, i)), _const_spec(bias_col.shape)],
        out_specs=(pair_spec, pair_spec, tab_spec, tab_spec),
        compiler_params=_params(("parallel",)),
        name="route",
    )(logits_t, bias_col)


def _chunk_copy(src_ref, src_chunk, dst_ref, dst_chunk, sem):
    return pltpu.make_async_copy(src_ref.at[src_chunk], dst_ref.at[dst_chunk], sem)


def _flat_tile():
    return pl.program_id(0) * pl.num_programs(1) + pl.program_id(1)


def _wait_slot(local_slot_ref, global_ref, sem, to_global):
    whole = global_ref.at[pl.ds(0, MAX_CHUNKS)]
    if to_global:
        pltpu.make_async_copy(local_slot_ref, whole, sem).wait()
    else:
        pltpu.make_async_copy(whole, local_slot_ref, sem).wait()


def _dispatch_kernel(dst_tab, tail_lo, tail_n,
                     x_ref, pos_ref, wts_ref, xs_hbm, sel_ref, loc, zbuf, sems):
    i = _flat_tile()
    n_tiles = pl.num_programs(0) * pl.num_programs(1)
    slot = i % 2
    pos = pos_ref[...]
    wts = wts_ref[...].astype(BF16)
    x = x_ref[0]
    cpb = ROW_TILE // CHUNK
    rid = lax.broadcasted_iota(jnp.int32, (ROW_TILE, ROW_TILE), 0).astype(F32).astype(BF16)
    for rb in range(LOCAL_ROWS // ROW_TILE):
        rel = pos - rb * ROW_TILE
        rel = jnp.where(rel < 0, -1, jnp.where(rel >= ROW_TILE, -1, rel)).astype(F32).astype(BF16)
        pw = jnp.zeros((ROW_TILE, ROW_TILE), BF16)
        for k in range(TOP_K):
            hit = rid == jnp.broadcast_to(rel[k:k + 1, :], rid.shape)
            pw = jnp.where(hit, jnp.broadcast_to(wts[k:k + 1, :], rid.shape), pw)
        p = jnp.where(pw > 0, jnp.ones_like(pw), jnp.zeros_like(pw))
        loc[slot, rb * cpb:(rb + 1) * cpb] = jnp.dot(
            p, x, preferred_element_type=F32).astype(BF16).reshape(cpb, CHUNK, x.shape[1])
        sel_ref[0, rb] = pw.T
        for c in range(rb * cpb, (rb + 1) * cpb):
            _chunk_copy(loc.at[slot], c, xs_hbm, dst_tab[i * MAX_CHUNKS + c], sems.at[slot]).start()

    @pl.when(i > 0)
    def _():
        _wait_slot(loc.at[1 - slot], xs_hbm, sems.at[1 - slot], to_global=True)

    @pl.when(i == n_tiles - 1)
    def _():
        _wait_slot(loc.at[slot], xs_hbm, sems.at[slot], to_global=True)
        zbuf[...] = jnp.zeros_like(zbuf)

        def per_expert(e, n_started):
            def per_chunk(c, carry):
                _chunk_copy(zbuf, 0, xs_hbm, tail_lo[e] + c, sems.at[slot]).start()
                return carry

            lax.fori_loop(0, tail_n[e], per_chunk, 0)
            return n_started + tail_n[e]

        n_started = lax.fori_loop(0, N_EXPERTS, per_expert, 0)

        def drain(c, carry):
            _chunk_copy(zbuf, 0, xs_hbm, 0, sems.at[slot]).wait()
            return carry

        lax.fori_loop(0, n_started, drain, 0)


def _dispatch(u2, pos, wts, tabs, n_row_tiles):
    n_batch, t, d = u2.shape
    tiles = t // ROW_TILE
    n_blocks = LOCAL_ROWS // ROW_TILE
    pair = pl.BlockSpec((TOP_K, ROW_TILE), lambda b, j, *_: (0, b * tiles + j))
    return pl.pallas_call(
        _dispatch_kernel,
        out_shape=(jax.ShapeDtypeStruct((_buffer_chunks(n_row_tiles), CHUNK, d), BF16),
                   jax.ShapeDtypeStruct((n_batch * tiles, n_blocks, ROW_TILE, ROW_TILE), BF16)),
        grid_spec=pltpu.PrefetchScalarGridSpec(
            num_scalar_prefetch=3,
            grid=(n_batch, tiles),
            in_specs=[pl.BlockSpec((1, ROW_TILE, d), lambda b, j, *_: (b, j, 0)), pair, pair],
            out_specs=(pl.BlockSpec(memory_space=pl.ANY),
                       pl.BlockSpec((1, n_blocks, ROW_TILE, ROW_TILE),
                                    lambda b, j, *_: (b * tiles + j, 0, 0, 0))),
            scratch_shapes=[pltpu.VMEM((2, MAX_CHUNKS, CHUNK, d), BF16), pltpu.VMEM((1, CHUNK, d), BF16),
                            pltpu.SemaphoreType.DMA((2,))],
        ),
        compiler_params=_params(("arbitrary", "arbitrary")),
        name="dispatch",
    )(tabs["dst"], tabs["tail_lo"], tabs["tail_n"], u2, pos, wts)


def _experts_kernel(te_tab, nv_tab, xs_ref, w1_ref, w3_ref, w2_ref, ys_ref, w1b, w3b, w2b):
    @pl.when(pl.program_id(0) < nv_tab[0])
    def _():
        j = pl.program_id(0)

        @pl.when((j == 0) | (te_tab[j] != te_tab[jnp.maximum(j - 1, 0)]))
        def _():
            w1b[...] = w1_ref[0, 0].astype(BF16)
            w3b[...] = w3_ref[0, 0].astype(BF16)
            w2b[...] = w2_ref[0, 0].astype(BF16)

        x = xs_ref[...]
        h1 = jnp.dot(x, w1b[...], preferred_element_type=F32)
        h3 = jnp.dot(x, w3b[...], preferred_element_type=F32)
        h = (_silu(h1) * h3).astype(BF16)
        ys_ref[...] = jnp.dot(h, w2b[...], preferred_element_type=F32).astype(BF16)


def _experts(xs, w1, w3, w2, layer, tile_expert, n_valid, n_row_tiles):
    rows, d = xs.shape
    de = w1.shape[3]
    row = pl.BlockSpec((EXPERT_ROW_TILE, d), lambda j, te, nv: (jnp.minimum(j, nv[0] - 1), 0))
    return pl.pallas_call(
        _experts_kernel,
        out_shape=jax.ShapeDtypeStruct((rows, d), BF16),
        grid_spec=pltpu.PrefetchScalarGridSpec(
            num_scalar_prefetch=2,
            grid=(n_row_tiles,),
            in_specs=[row,
                      pl.BlockSpec((1, 1, d, de), lambda j, te, nv: (layer, te[j], 0, 0)),
                      pl.BlockSpec((1, 1, d, de), lambda j, te, nv: (layer, te[j], 0, 0)),
                      pl.BlockSpec((1, 1, de, d), lambda j, te, nv: (layer, te[j], 0, 0))],
            out_specs=row,
            scratch_shapes=[pltpu.VMEM((d, de), BF16), pltpu.VMEM((d, de), BF16),
                            pltpu.VMEM((de, d), BF16)],
        ),
        compiler_params=_params(("arbitrary",)),
        name="experts",
    )(tile_expert, n_valid, xs, w1, w3, w2)


def _combine_kernel(src_tab,
                    ys_hbm, sel_ref, x1_ref, u2_ref, mod_ref, sw1_ref, sw3_ref, sw2_ref, g_ref, b_ref,
                    o_ref, loc, sems, *, n_batch, n_ctx_tiles, d, alpha):
    i = _flat_tile()
    n_tiles = pl.num_programs(0) * pl.num_programs(1)
    slot = i % COMBINE_SLOTS
    ahead = COMBINE_SLOTS - 1

    def fetch(tile, sl, chunks):
        for c in chunks:
            _chunk_copy(ys_hbm, src_tab[tile * MAX_CHUNKS + c], loc.at[sl], c, sems.at[sl]).start()

    @pl.when(i == 0)
    def _():
        for s in range(ahead):
            fetch(s % n_tiles, s, range(MAX_CHUNKS))

    _wait_slot(loc.at[slot], ys_hbm, sems.at[slot], to_global=False)

    nxt = (i + ahead) % n_tiles
    nxt_slot = (i + ahead) % COMBINE_SLOTS
    cpb = ROW_TILE // CHUNK
    routed = jnp.zeros((ROW_TILE, d), F32)
    for rb in range(LOCAL_ROWS // ROW_TILE):
        rows = loc[slot, rb * cpb:(rb + 1) * cpb].reshape(ROW_TILE, d)
        routed = routed + jnp.dot(sel_ref[0, rb], rows, preferred_element_type=F32)
        fetch(nxt, nxt_slot, range(rb * cpb, (rb + 1) * cpb))

    u = u2_ref[0]
    hs = (_silu(jnp.dot(u, sw1_ref[...], preferred_element_type=F32))
          * jnp.dot(u, sw3_ref[...], preferred_element_type=F32)).astype(BF16)
    shared = jnp.dot(hs, sw2_ref[...], preferred_element_type=F32)

    m = _mod_row(mod_ref, n_batch, n_ctx_tiles)
    g2 = m[:, 5 * d:6 * d]
    o_ref[0] = _layer_norm(alpha * x1_ref[0] + g2 * (routed + shared), g_ref[...], b_ref[...])

    @pl.when(i == n_tiles - 1)
    def _():
        for s in range(1, COMBINE_SLOTS):
            sl = (i + s) % COMBINE_SLOTS
            _wait_slot(loc.at[sl], ys_hbm, sems.at[sl], to_global=False)


def _combine(ys, sel, x1, u2, mod_l, sw1, sw3, sw2, ln_g, ln_b, tabs, n_ctx, alpha):
    n_batch, t, d = x1.shape
    tiles = t // ROW_TILE
    n_blocks = LOCAL_ROWS // ROW_TILE
    row = pl.BlockSpec((1, ROW_TILE, d), lambda b, j, *_: (b, j, 0))
    return pl.pallas_call(
        functools.partial(_combine_kernel, n_batch=n_batch, n_ctx_tiles=n_ctx // ROW_TILE, d=d,
                          alpha=alpha),
        out_shape=jax.ShapeDtypeStruct((n_batch, t, d), F32),
        grid_spec=pltpu.PrefetchScalarGridSpec(
            num_scalar_prefetch=1,
            grid=(n_batch, tiles),
            in_specs=[pl.BlockSpec(memory_space=pl.ANY),
                      pl.BlockSpec((1, n_blocks, ROW_TILE, ROW_TILE),
                                   lambda b, j, *_: (b * tiles + j, 0, 0, 0)),
                      row, row] + [_const_spec(a.shape) for a in (mod_l, sw1, sw3, sw2, ln_g, ln_b)],
            out_specs=row,
            scratch_shapes=[pltpu.VMEM((COMBINE_SLOTS, MAX_CHUNKS, CHUNK, d), BF16),
                            pltpu.SemaphoreType.DMA((COMBINE_SLOTS,))],
        ),
        compiler_params=_params(("arbitrary", "arbitrary")),
        name="combine",
    )(tabs["src"], ys, sel, x1, u2, mod_l, sw1, sw3, sw2, ln_g, ln_b)


def _dispatch_tables(nch, lo, n_row_tiles):
    cpt = EXPERT_ROW_TILE // CHUNK
    tot_e = jnp.sum(nch, axis=0)
    region = (tot_e + cpt - 1) // cpt * cpt
    ends = jnp.cumsum(region)
    base = ends - region
    g = base[None, :] + jnp.cumsum(nch, axis=0) - nch
    c = jnp.arange(MAX_CHUNKS, dtype=nch.dtype)[None, :, None]
    covers = (lo[:, None, :] <= c) & (c < (lo + nch)[:, None, :])
    used = jnp.any(covers, axis=-1)
    dst = c[:, :, 0] + jnp.sum(jnp.where(covers, (g - lo)[:, None, :], 0), axis=-1)
    slot = (jnp.arange(nch.shape[0], dtype=nch.dtype) % 2)[:, None]
    spare = n_row_tiles * cpt + slot * MAX_CHUNKS + c[:, :, 0]
    row_tile = jnp.arange(n_row_tiles, dtype=nch.dtype)[:, None]
    tile_expert = jnp.sum((ends // cpt)[None, :] <= row_tile, axis=-1)
    i32 = lambda a: a.astype(jnp.int32)
    tabs = dict(dst=i32(jnp.where(used, dst, spare).reshape(-1)),
                src=i32(jnp.where(used, dst, 0).reshape(-1)),
                tail_lo=i32(base + tot_e), tail_n=i32(region - tot_e))
    return tabs, i32(jnp.minimum(tile_expert, N_EXPERTS - 1)), i32(ends[-1:] // cpt)


def _buffer_chunks(n_row_tiles):
    return n_row_tiles * (EXPERT_ROW_TILE // CHUNK) + 2 * MAX_CHUNKS


def _rope_tables(n_ctx, n_lat):
    pos = jnp.arange(n_lat)
    n_freq = HEAD_DIM // 4
    inv = ROPE_THETA ** (-jnp.arange(n_freq, dtype=F32) / n_freq)
    ang_r = (pos // GRID_W).astype(F32)[:, None] * inv
    ang_c = (pos % GRID_W).astype(F32)[:, None] * inv
    cos_l = jnp.concatenate([jnp.cos(ang_r)] * 2 + [jnp.cos(ang_c)] * 2, axis=1)
    sin_l = jnp.concatenate([-jnp.sin(ang_r), jnp.sin(ang_r), -jnp.sin(ang_c), jnp.sin(ang_c)], axis=1)
    cos_t = jnp.concatenate([jnp.ones((n_ctx, HEAD_DIM), F32), cos_l], axis=0)
    sin_t = jnp.concatenate([jnp.zeros((n_ctx, HEAD_DIM), F32), sin_l], axis=0)
    return cos_t, sin_t


def _pack_gate_weights(wa, wi):
    n_dir, n_blocks, w, _ = wa.shape
    per_tile = MXU_TILE // w
    n_tiles = n_blocks // per_tile

    def dense(wx):
        wx = wx.reshape(n_dir, n_tiles, per_tile, w, w)
        eye = jnp.eye(per_tile, dtype=wx.dtype)
        full = jnp.einsum('dtpij,pq->dtpiqj', wx, eye)
        return full.reshape(n_dir, n_tiles, MXU_TILE, MXU_TILE)

    return jnp.concatenate([dense(wa), dense(wi)], axis=-1).astype(BF16)


def kernel(x, c, ctx, c_ctx, w_mod, b_mod, w_in, sink, conv_w, conv_b, rg_wa, rg_ba, rg_wi, rg_bi,
           rg_lambda, w_o_attn, w_o_rnn, w_out, ln1_g, ln1_b, router_w, router_bias, exp_w1, exp_w3,
           exp_w2, sh_w1, sh_w3, sh_w2, ln2_g, ln2_b):
    n_batch, n_lat, d = x.shape
    n_ctx = ctx.shape[1]
    n_layers = w_mod.shape[0]
    t = n_ctx + n_lat
    assert n_ctx % ROW_TILE == 0 and n_lat % ROW_TILE == 0 and n_batch + 1 <= MOD_ROWS
    assert exp_w1.shape[1:] == (N_EXPERTS, d, D_EXPERT) and sh_w1.shape[1:] == (d, D_EXPERT)
    alpha = (2 * n_layers) ** 0.25
    n_tiles = n_batch * t // ROW_TILE
    chunks_per_row_tile = EXPERT_ROW_TILE // CHUNK
    max_chunks = (TOP_K * n_batch * t // CHUNK + n_tiles * N_EXPERTS
                  + N_EXPERTS * (chunks_per_row_tile - 1))
    n_row_tiles = pl.cdiv(max_chunks, chunks_per_row_tile)

    xa = jnp.concatenate([ctx, x], axis=1)
    cc = jnp.zeros((MOD_ROWS, d), F32).at[:n_batch].set(c).at[n_batch].set(c_ctx)
    mod = _modulation(cc, w_mod, b_mod)
    cos_t, sin_t = _rope_tables(n_ctx, n_lat)
    mask_bias = _attn_mask_bias(n_ctx)

    for l in range(n_layers):
        mod_l = mod[l]
        q, k, v, xr, gy, sa, sr = _in_projection(xa, mod_l, w_in[l].astype(BF16), cos_t, sin_t, n_ctx)
        attn = _attention(sink[l], q, k, v, mask_bias, n_ctx)
        wg = _pack_gate_weights(rg_wa[l], rg_wi[l])
        hf, hb = _rnn_branch(xr, conv_w[l], conv_b[l][None, :], wg, rg_ba[l], rg_bi[l], rg_lambda[l],
                             n_ctx)
        rw_t = router_w[l].T
        rw_hi = rw_t.astype(BF16)
        rw_lo = (rw_t - rw_hi.astype(F32)).astype(BF16)
        x1, u2, logits_t = _out_projection(
            xa, attn, hf, hb, gy, sa, sr, mod_l, w_o_attn[l].astype(BF16), w_o_rnn[l].astype(BF16),
            w_out[l].astype(BF16), ln1_g[l][None, :], ln1_b[l][None, :], rw_hi, rw_lo, n_ctx, alpha)
        pos, wts, nch, lo = _route(logits_t, router_bias[l][:, None])
        tabs, tile_expert, n_valid = _dispatch_tables(nch[:, :, 0], lo[:, :, 0], n_row_tiles)
        xs, sel = _dispatch(u2, pos, wts, tabs, n_row_tiles)
        ys = _experts(xs.reshape(-1, d), exp_w1, exp_w3, exp_w2, l, tile_expert, n_valid, n_row_tiles)
        xa = _combine(ys.reshape(xs.shape), sel, x1, u2, mod_l, sh_w1[l].astype(BF16), sh_w3[l].astype(BF16),
                      sh_w2[l].astype(BF16), ln2_g[l][None, :], ln2_b[l][None, :], tabs, n_ctx, alpha)
    return xa[:, n_ctx:, :]
```

```python
import functools
import math

import jax
import jax.numpy as jnp
from jax import lax
from jax.experimental import pallas as pl
from jax.experimental.pallas import tpu as pltpu

N_HEADS = 8
N_KV_HEADS = 2
HEAD_DIM = 128
KV_GROUP = N_HEADS // N_KV_HEADS
ATT_BLOCK = 128
GRID_W = 64
ROPE_THETA = 10000.0
RNN_BLOCKS = 16
LRU_C = 8.0
N_EXPERTS = 64
TOP_K = 8
N_GROUPS = 8
TOPK_GROUPS = 4
D_EXPERT = 256
ROUTE_SCALE = 2.5
LN_EPS = 1e-6

ROW_TILE = 256
MXU_TILE = 256
CHUNK = 16
LOCAL_ROWS = TOP_K * ROW_TILE + N_EXPERTS * CHUNK
MAX_CHUNKS = LOCAL_ROWS // CHUNK
EXPERT_ROW_TILE = 1024
COMBINE_SLOTS = 3
MOD_ROWS = 24
VMEM_LIMIT = 56 * 1024 * 1024

F32 = jnp.float32
BF16 = jnp.bfloat16
NEG_BIG = -1e30


def _const_spec(shape):
    zeros = (0,) * len(shape)
    return pl.BlockSpec(shape, lambda *_: zeros, pipeline_mode=pl.Buffered(1))


def _params(sem):
    return pltpu.CompilerParams(dimension_semantics=sem, vmem_limit_bytes=VMEM_LIMIT)


def _silu(v):
    return v * jax.nn.sigmoid(v)


def _gelu_tanh(v):
    return v * (0.5 * (1.0 + jnp.tanh(math.sqrt(2.0 / math.pi) * (v + 0.044715 * (v * v * v)))))


def _layer_norm(v, g, b):
    mu = jnp.mean(v, axis=-1, keepdims=True)
    d = v - mu
    var = jnp.mean(d * d, axis=-1, keepdims=True)
    return d * lax.rsqrt(var + LN_EPS) * g + b


def _mod_kernel(cc_ref, w_ref, b_ref, o_ref):
    s = _silu(cc_ref[...]).astype(BF16)
    o_ref[0] = jnp.dot(s, w_ref[0].astype(BF16), preferred_element_type=F32) + b_ref[0]


def _modulation(cc, w_mod, b_mod):
    n_layers, d, d6 = w_mod.shape
    col = 1536
    return pl.pallas_call(
        _mod_kernel,
        out_shape=jax.ShapeDtypeStruct((n_layers, MOD_ROWS, d6), F32),
        grid=(n_layers, d6 // col),
        in_specs=[
            pl.BlockSpec((MOD_ROWS, d), lambda l, j: (0, 0)),
            pl.BlockSpec((1, d, col), lambda l, j: (l, 0, j)),
            pl.BlockSpec((1, 1, col), lambda l, j: (l, 0, j)),
        ],
        out_specs=pl.BlockSpec((1, MOD_ROWS, col), lambda l, j: (l, 0, j)),
        compiler_params=_params(("parallel", "parallel")),
        name="modulation",
    )(cc, w_mod, b_mod.reshape(n_layers, 1, d6))


def _tiles_per_step(tiles):
    return next(s for s in (3, 2, 1) if tiles % s == 0)


def _mod_row(mod_ref, n_batch, n_ctx_tiles):
    b = pl.program_id(0)
    j = pl.program_id(1)
    row = jnp.where(j < n_ctx_tiles, n_batch, b)
    return mod_ref[pl.ds(row, 1), :]


def _rope(v, cos_w, sin_w):
    width = v.shape[1]
    lane = lax.broadcasted_iota(jnp.int32, v.shape, 1)
    partner = jnp.where((lane & 32) == 0, pltpu.roll(v, width - 32, 1), pltpu.roll(v, 32, 1))
    return v * cos_w + partner * sin_w


def _inproj_kernel(x_ref, mod_ref, w_ref, cos_ref, sin_ref,
                   q_ref, k_ref, v_ref, xr_ref, gy_ref, sa_ref, sr_ref,
                   *, n_batch, n_ctx_tiles, d, sub):
    b = pl.program_id(0)
    j = pl.program_id(1)
    parts = []
    for s in range(sub):
        mrow = jnp.where(j * sub + s < n_ctx_tiles, n_batch, b)
        m = mod_ref[pl.ds(mrow, 1), :]
        xs = x_ref[0, s * ROW_TILE:(s + 1) * ROW_TILE, :]
        parts.append((xs * (1.0 + m[:, d:2 * d]) + m[:, 0:d]).astype(BF16))
    u = jnp.concatenate(parts, axis=0)

    def proj(lo, hi):
        return jnp.dot(u, w_ref[:, lo:hi], preferred_element_type=F32)

    qw = N_HEADS * HEAD_DIM
    kw = N_KV_HEADS * HEAD_DIM
    cos1 = cos_ref[...]
    sin1 = sin_ref[...]
    o = 0
    q = proj(o, o + qw) * (HEAD_DIM ** -0.5)
    q_ref[0] = _rope(q, jnp.concatenate([cos1] * N_HEADS, axis=1),
                     jnp.concatenate([sin1] * N_HEADS, axis=1)).astype(BF16)
    o += qw
    k = proj(o, o + kw)
    k_ref[0] = _rope(k, jnp.concatenate([cos1] * N_KV_HEADS, axis=1),
                     jnp.concatenate([sin1] * N_KV_HEADS, axis=1)).astype(BF16)
    o += kw
    v_ref[0] = proj(o, o + kw).astype(BF16)
    o += kw
    xr_ref[0] = proj(o, o + d)
    o += d
    gy_ref[0] = _gelu_tanh(proj(o, o + d)).astype(BF16)
    o += d
    sa_ref[0] = jax.nn.sigmoid(proj(o, o + d)).astype(BF16)
    o += d
    sr_ref[0] = jax.nn.sigmoid(proj(o, o + d)).astype(BF16)


def _in_projection(xa, mod_l, w_in, cos_t, sin_t, n_ctx):
    n_batch, t, d = xa.shape
    qw = N_HEADS * HEAD_DIM
    kw = N_KV_HEADS * HEAD_DIM
    tiles = t // ROW_TILE
    sub = _tiles_per_step(tiles)
    row = lambda w: pl.BlockSpec((1, sub * ROW_TILE, w), lambda b, j: (b, j, 0))
    tab = pl.BlockSpec((sub * ROW_TILE, HEAD_DIM), lambda b, j: (j, 0))
    shp = lambda w, dt: jax.ShapeDtypeStruct((n_batch, t, w), dt)
    return pl.pallas_call(
        functools.partial(_inproj_kernel, n_batch=n_batch, n_ctx_tiles=n_ctx // ROW_TILE, d=d,
                          sub=sub),
        out_shape=(shp(qw, BF16), shp(kw, BF16), shp(kw, BF16), shp(d, F32),
                   shp(d, BF16), shp(d, BF16), shp(d, BF16)),
        grid=(n_batch, tiles // sub),
        in_specs=[row(d), _const_spec(mod_l.shape), _const_spec(w_in.shape), tab, tab],
        out_specs=(row(qw), row(kw), row(kw), row(d), row(d), row(d), row(d)),
        compiler_params=_params(("parallel", "parallel")),
        name="in_projection",
    )(xa, mod_l, w_in, cos_t, sin_t)


def _attn_mask_bias(n_ctx):
    n_win = 3 * ATT_BLOCK
    rr = jnp.arange(ATT_BLOCK)[:, None]
    jj = jnp.arange(n_win + n_ctx)[None, :]
    in_band = (jj >= rr) & (jj <= rr + 2 * ATT_BLOCK)
    is_ctx = jj >= n_win
    kinds = [is_ctx | jnp.zeros_like(in_band)]
    for has_left, has_right in ((False, False), (True, False), (False, True), (True, True)):
        ok = in_band & (has_left | (jj >= ATT_BLOCK)) & (has_right | (jj < 2 * ATT_BLOCK))
        kinds.append(is_ctx | ok)
    return jnp.where(jnp.stack(kinds), 0.0, NEG_BIG).astype(F32)


def _attn_kernel(sink_ref, q_ref, kp_ref, kc_ref, kn_ref, kx_ref, vp_ref, vc_ref, vn_ref, vx_ref,
                 bias_ref, o_ref, *, n_ctx_blocks, n_lat_blocks):
    i = pl.program_id(1) - n_ctx_blocks
    kind = jnp.where(i < 0, 0, 1 + (i > 0).astype(jnp.int32) + 2 * (i < n_lat_blocks - 1).astype(jnp.int32))
    rows = KV_GROUP * ATT_BLOCK
    bias1 = bias_ref[kind]
    bias = jnp.concatenate([bias1] * KV_GROUP, axis=0)
    rblk = lax.broadcasted_iota(jnp.int32, (rows, 1), 0) // ATT_BLOCK
    q = q_ref[0]
    for h in range(N_KV_HEADS):
        hs = slice(h * HEAD_DIM, (h + 1) * HEAD_DIM)
        kk = jnp.concatenate([kp_ref[0, :, hs], kc_ref[0, :, hs], kn_ref[0, :, hs], kx_ref[0, :, hs]], axis=0)
        vv = jnp.concatenate([vp_ref[0, :, hs], vc_ref[0, :, hs], vn_ref[0, :, hs], vx_ref[0, :, hs]], axis=0)
        heads = [h * KV_GROUP + g for g in range(KV_GROUP)]
        qs = jnp.concatenate([q[:, hd * HEAD_DIM:(hd + 1) * HEAD_DIM] for hd in heads], axis=0)
        s = lax.dot_general(qs, kk, (((1,), (1,)), ((), ())), preferred_element_type=F32) + bias
        sink = jnp.zeros((rows, 1), F32)
        for g, hd in enumerate(heads):
            sink = jnp.where(rblk == g, sink_ref[hd], sink)
        mx = jnp.maximum(jnp.max(s, axis=1, keepdims=True), sink)
        p = jnp.exp(s - mx)
        den = jnp.sum(p, axis=1, keepdims=True) + jnp.exp(sink - mx)
        o = jnp.dot(p.astype(BF16), vv, preferred_element_type=F32) / den
        for g, hd in enumerate(heads):
            o_ref[0, :, hd * HEAD_DIM:(hd + 1) * HEAD_DIM] = o[g * ATT_BLOCK:(g + 1) * ATT_BLOCK].astype(BF16)


def _attention(sink_l, q, k, v, mask_bias, n_ctx):
    n_batch, t, qw = q.shape
    kw = k.shape[2]
    ncb = n_ctx // ATT_BLOCK
    nlb = (t - n_ctx) // ATT_BLOCK

    def lat_blk(off):
        def index(b, j, sink):
            return (b, ncb + jnp.clip(j - ncb + off, 0, nlb - 1), 0)
        return pl.BlockSpec((1, ATT_BLOCK, kw), index)

    ctx_blk = pl.BlockSpec((1, n_ctx, kw), lambda b, j, sink: (b, 0, 0))
    q_blk = pl.BlockSpec((1, ATT_BLOCK, qw), lambda b, j, sink: (b, j, 0))
    kv_specs = [lat_blk(-1), lat_blk(0), lat_blk(1), ctx_blk]
    return pl.pallas_call(
        functools.partial(_attn_kernel, n_ctx_blocks=ncb, n_lat_blocks=nlb),
        out_shape=jax.ShapeDtypeStruct((n_batch, t, qw), BF16),
        grid_spec=pltpu.PrefetchScalarGridSpec(
            num_scalar_prefetch=1,
            grid=(n_batch, t // ATT_BLOCK),
            in_specs=[q_blk] + kv_specs + kv_specs + [_const_spec(mask_bias.shape)],
            out_specs=q_blk,
        ),
        compiler_params=_params(("parallel", "parallel")),
        name="attention",
    )(sink_l, q, k, k, k, k, v, v, v, v, mask_bias)


def _rnn_kernel(x_ref, cw_ref, cb_ref, wg_ref, ba_ref, bi_ref, lam_ref, hf_ref, hb_ref,
                a_scr, b_scr, h_scr, *, n_ctx, n_rows):
    j = pl.program_id(1)
    tb = ROW_TILE
    n_blk = n_rows // tb
    n_cblk = n_ctx // tb
    blk_f = j
    blk_b = jnp.where(j < n_cblk, n_cblk - 1 - j, n_blk - 1 - (j - n_cblk))
    d = x_ref.shape[2]
    ridx = lax.broadcasted_iota(jnp.int32, (tb, d), 0)

    def conv_block(blk):
        t0 = pl.multiple_of(blk * tb, tb)
        cur = x_ref[0, pl.ds(t0, tb), :]
        prev8 = x_ref[0, pl.ds(pl.multiple_of(jnp.maximum(t0 - 8, 0), 8), 8), :]
        next8 = x_ref[0, pl.ds(pl.multiple_of(jnp.minimum(t0 + tb, n_rows - 8), 8), 8), :]
        prev_ok = (t0 != 0) & (t0 != n_ctx)
        next_ok = (t0 + tb != n_ctx) & (t0 + tb != n_rows)
        prev8 = jnp.where(prev_ok, prev8, 0.0)
        next8 = jnp.where(next_ok, next8, 0.0)
        p6 = jnp.broadcast_to(prev8[6:7, :], (tb, d))
        p7 = jnp.broadcast_to(prev8[7:8, :], (tb, d))
        n0 = jnp.broadcast_to(next8[0:1, :], (tb, d))
        xm1 = jnp.where(ridx == 0, p7, pltpu.roll(cur, 1, 0))
        xm2 = jnp.where(ridx == 0, p6, jnp.where(ridx == 1, p7, pltpu.roll(cur, 2, 0)))
        xp1 = jnp.where(ridx == tb - 1, n0, pltpu.roll(cur, tb - 1, 0))
        return (cb_ref[...] + xm2 * cw_ref[0:1, :] + xm1 * cw_ref[1:2, :]
                + cur * cw_ref[2:3, :] + xp1 * cw_ref[3:4, :])

    def gate_block(xc, dr):
        xb = xc.astype(BF16)
        r_parts, i_parts = [], []
        for jt in range(d // MXU_TILE):
            g = jnp.dot(xb[:, jt * MXU_TILE:(jt + 1) * MXU_TILE], wg_ref[dr, jt],
                        preferred_element_type=F32)
            r_parts.append(g[:, :MXU_TILE])
            i_parts.append(g[:, MXU_TILE:])
        r = jax.nn.sigmoid(jnp.concatenate(r_parts, axis=1) + ba_ref[dr:dr + 1, :])
        ig = jax.nn.sigmoid(jnp.concatenate(i_parts, axis=1) + bi_ref[dr:dr + 1, :])
        nl = -lam_ref[dr:dr + 1, :]
        softplus = jnp.maximum(nl, 0.0) + jnp.log1p(jnp.exp(-jnp.abs(nl)))
        log_a = (-LRU_C) * r * softplus
        a = jnp.exp(log_a)
        a_scr[dr] = a
        gap = 1.0 - a * a
        root = jnp.where(gap > 0.0, gap * lax.rsqrt(gap), 0.0)
        b_scr[dr] = root * (ig * xc)

    gate_block(conv_block(blk_f), 0)
    gate_block(conv_block(blk_b), 1)

    @pl.when(j == 0)
    def _():
        h_scr[...] = jnp.zeros_like(h_scr)

    def step(s, carry):
        hf, hb = carry
        tf = s
        tr = tb - 1 - s
        hf = a_scr[0, pl.ds(tf, 1), :] * hf + b_scr[0, pl.ds(tf, 1), :]
        hb = a_scr[1, pl.ds(tr, 1), :] * hb + b_scr[1, pl.ds(tr, 1), :]
        hf_ref[0, pl.ds(tf, 1), :] = hf
        hb_ref[0, pl.ds(tr, 1), :] = hb
        return hf, hb

    hf, hb = lax.fori_loop(0, tb, step, (h_scr[0:1, :], h_scr[1:2, :]), unroll=8)
    h_scr[0:1, :] = hf
    h_scr[1:2, :] = hb


def _rnn_branch(xr, conv_w, conv_b, wg, ba, bi, lam, n_ctx):
    n_batch, t, d = xr.shape
    n_blk = t // ROW_TILE
    n_cblk = n_ctx // ROW_TILE

    def bwd_index(b, j):
        return (b, jnp.where(j < n_cblk, n_cblk - 1 - j, n_blk - 1 - (j - n_cblk)), 0)

    out = jax.ShapeDtypeStruct((n_batch, t, d), F32)
    return pl.pallas_call(
        functools.partial(_rnn_kernel, n_ctx=n_ctx, n_rows=t),
        out_shape=(out, out),
        grid=(n_batch, n_blk),
        in_specs=[
            pl.BlockSpec((1, t, d), lambda b, j: (b, 0, 0)),
            _const_spec(conv_w.shape), _const_spec(conv_b.shape), _const_spec(wg.shape),
            _const_spec(ba.shape), _const_spec(bi.shape), _const_spec(lam.shape),
        ],
        out_specs=(pl.BlockSpec((1, ROW_TILE, d), lambda b, j: (b, j, 0)),
                   pl.BlockSpec((1, ROW_TILE, d), bwd_index)),
        scratch_shapes=[pltpu.VMEM((2, ROW_TILE, d), F32), pltpu.VMEM((2, ROW_TILE, d), F32),
                        pltpu.VMEM((8, d), F32)],
        compiler_params=_params(("parallel", "arbitrary")),
        name="rnn_branch",
    )(xr, conv_w, conv_b, wg, ba, bi, lam)


def _dot_nt(a, b):
    return lax.dot_general(a, b, (((1,), (1,)), ((), ())), preferred_element_type=F32)


def _outproj_kernel(x_ref, attn_ref, hf_ref, hb_ref, gy_ref, sa_ref, sr_ref, mod_ref,
                    woa_ref, wor_ref, wout_ref, g_ref, b_ref, rwh_ref, rwl_ref,
                    x1_ref, u2_ref, lg_ref, *, n_batch, n_ctx_tiles, d, alpha, sub):
    b = pl.program_id(0)
    j = pl.program_id(1)
    for s in range(sub):
        rs = slice(s * ROW_TILE, (s + 1) * ROW_TILE)
        mrow = jnp.where(j * sub + s < n_ctx_tiles, n_batch, b)
        m = mod_ref[pl.ds(mrow, 1), :]
        g1 = m[:, 2 * d:3 * d]
        sh2 = m[:, 3 * d:4 * d]
        sc2 = m[:, 4 * d:5 * d]
        a = jnp.dot(attn_ref[0, rs, :], woa_ref[...], preferred_element_type=F32)
        rnn = ((hf_ref[0, rs, :] + hb_ref[0, rs, :]) * gy_ref[0, rs, :].astype(F32)).astype(BF16)
        r = jnp.dot(rnn, wor_ref[...], preferred_element_type=F32)
        merged = (sa_ref[0, rs, :].astype(F32) * a + sr_ref[0, rs, :].astype(F32) * r).astype(BF16)
        mix = jnp.dot(merged, wout_ref[...], preferred_element_type=F32)
        x1 = _layer_norm(alpha * x_ref[0, rs, :] + g1 * mix, g_ref[...], b_ref[...])
        x1_ref[0, rs, :] = x1
        u2 = x1 * (1.0 + sc2) + sh2
        u_hi = u2.astype(BF16)
        u2_ref[0, rs, :] = u_hi
        u_lo = (u2 - u_hi.astype(F32)).astype(BF16)
        lg_ref[:, rs] = (_dot_nt(rwh_ref[...], u_hi) + _dot_nt(rwh_ref[...], u_lo)
                         + _dot_nt(rwl_ref[...], u_hi))


def _out_projection(xa, attn, hf, hb, gy, sa, sr, mod_l, woa, wor, wout, ln_g, ln_b, rwh, rwl,
                    n_ctx, alpha):
    n_batch, t, d = xa.shape
    tiles = t // ROW_TILE
    sub = _tiles_per_step(tiles)
    steps = tiles // sub
    row = pl.BlockSpec((1, sub * ROW_TILE, d), lambda b, j: (b, j, 0))
    return pl.pallas_call(
        functools.partial(_outproj_kernel, n_batch=n_batch, n_ctx_tiles=n_ctx // ROW_TILE, d=d,
                          alpha=alpha, sub=sub),
        out_shape=(jax.ShapeDtypeStruct((n_batch, t, d), F32),
                   jax.ShapeDtypeStruct((n_batch, t, d), BF16),
                   jax.ShapeDtypeStruct((N_EXPERTS, n_batch * t), F32)),
        grid=(n_batch, steps),
        in_specs=[row] * 7 + [_const_spec(a.shape) for a in
                              (mod_l, woa, wor, wout, ln_g, ln_b, rwh, rwl)],
        out_specs=(row, row,
                   pl.BlockSpec((N_EXPERTS, sub * ROW_TILE), lambda b, j: (0, b * steps + j))),
        compiler_params=_params(("parallel", "parallel")),
        name="out_projection",
    )(xa, attn, hf, hb, gy, sa, sr, mod_l, woa, wor, wout, ln_g, ln_b, rwh, rwl)


def _route_kernel(lg_ref, bias_ref, pos_ref, wts_ref, nch_ref, lo_ref):
    lg = lg_ref[...]
    n_tok = lg.shape[1]
    per_group = N_EXPERTS // N_GROUPS
    scores = jax.nn.sigmoid(lg)
    biased = scores + bias_ref[...]
    b3 = biased.reshape(N_GROUPS, per_group, n_tok)
    sub = lax.broadcasted_iota(jnp.int32, b3.shape, 1)
    m1 = jnp.max(b3, axis=1, keepdims=True)
    i1 = jnp.min(jnp.where(b3 == m1, sub, per_group), axis=1, keepdims=True)
    m2 = jnp.max(jnp.where(sub == i1, -jnp.inf, b3), axis=1, keepdims=True)
    grp = (m1 + m2).reshape(N_GROUPS, n_tok)

    gi = lax.broadcasted_iota(jnp.int32, grp.shape, 0)
    gsel = jnp.zeros(grp.shape, F32)
    for _ in range(TOPK_GROUPS):
        m = jnp.max(grp, axis=0, keepdims=True)
        idx = jnp.min(jnp.where(grp == m, gi, N_GROUPS), axis=0, keepdims=True)
        hit = gi == idx
        gsel = jnp.where(hit, 1.0, gsel)
        grp = jnp.where(hit, -jnp.inf, grp)
    emask = jnp.broadcast_to(gsel.reshape(N_GROUPS, 1, n_tok), b3.shape).reshape(N_EXPERTS, n_tok)
    cand = jnp.where(emask > 0.0, biased, -jnp.inf)

    ei = lax.broadcasted_iota(jnp.int32, cand.shape, 0)
    comb = jnp.zeros(cand.shape, F32)
    picked = jnp.zeros(cand.shape, F32)
    hits = []
    for _ in range(TOP_K):
        m = jnp.max(cand, axis=0, keepdims=True)
        idx = jnp.min(jnp.where(cand == m, ei, N_EXPERTS), axis=0, keepdims=True)
        hit = ei == idx
        hits.append(hit)
        comb = jnp.where(hit, scores, comb)
        picked = jnp.where(hit, 1.0, picked)
        cand = jnp.where(hit, -jnp.inf, cand)
    comb = comb / jnp.sum(comb, axis=0, keepdims=True) * ROUTE_SCALE

    cnt = jnp.sum(picked, axis=1, keepdims=True)
    nch = jnp.broadcast_to(jnp.floor((cnt + (CHUNK - 1)) * (1.0 / CHUNK)), (N_EXPERTS, 128))
    erow = lax.broadcasted_iota(jnp.int32, nch.shape, 0)
    incl = nch
    for s in (1, 2, 4, 8, 16, 32):
        incl = incl + jnp.where(erow >= s, pltpu.roll(incl, s, 0), 0.0)
    lo = incl - nch
    earlier = (lax.broadcasted_iota(jnp.int32, (n_tok, n_tok), 0)
               < lax.broadcasted_iota(jnp.int32, (n_tok, n_tok), 1)).astype(BF16)
    rank = jnp.dot(picked.astype(BF16), earlier, preferred_element_type=F32)
    slot = lo[:, 0:1] * CHUNK + rank
    pos = jnp.concatenate([jnp.sum(jnp.where(h, slot, 0.0), axis=0, keepdims=True) for h in hits], axis=0)
    wts = jnp.concatenate([jnp.sum(jnp.where(h, comb, 0.0), axis=0, keepdims=True) for h in hits], axis=0)
    pos_ref[...] = pos.astype(jnp.int32)
    wts_ref[...] = wts
    nch_ref[0] = nch.astype(jnp.int32)
    lo_ref[0] = lo.astype(jnp.int32)


def _route(logits_t, bias_col):
    n_tok = logits_t.shape[1]
    n_tiles = n_tok // ROW_TILE
    tab = jax.ShapeDtypeStruct((n_tiles, N_EXPERTS, 128), jnp.int32)
    tab_spec = pl.BlockSpec((1, N_EXPERTS, 128), lambda i: (i, 0, 0))
    pair_spec = pl.BlockSpec((TOP_K, ROW_TILE), lambda i: (0, i))
    return pl.pallas_call(
        _route_kernel,
        out_shape=(jax.ShapeDtypeStruct((TOP_K, n_tok), jnp.int32),
                   jax.ShapeDtypeStruct((TOP_K, n_tok), F32), tab, tab),
        grid=(n_tiles,),
        in_specs=[pl.BlockSpec((N_EXPERTS, ROW_TILE), lambda i: (0, i)), _const_spec(bias_col.shape)],
        out_specs=(pair_spec, pair_spec, tab_spec, tab_spec),
        compiler_params=_params(("parallel",)),
        name="route",
    )(logits_t, bias_col)


def _chunk_copy(src_ref, src_chunk, dst_ref, dst_chunk, sem):
    return pltpu.make_async_copy(src_ref.at[src_chunk], dst_ref.at[dst_chunk], sem)


def _flat_tile():
    return pl.program_id(0) * pl.num_programs(1) + pl.program_id(1)


def _wait_slot(local_slot_ref, global_ref, sem, to_global):
    whole = global_ref.at[pl.ds(0, MAX_CHUNKS)]
    if to_global:
        pltpu.make_async_copy(local_slot_ref, whole, sem).wait()
    else:
        pltpu.make_async_copy(whole, local_slot_ref, sem).wait()


def _selection_block(pos, vals, rb):
    rid = lax.broadcasted_iota(jnp.int32, (ROW_TILE, ROW_TILE), 0).astype(F32).astype(BF16)
    rel = pos - rb * ROW_TILE
    rel = jnp.where(rel < 0, -1, jnp.where(rel >= ROW_TILE, -1, rel)).astype(F32).astype(BF16)
    out = jnp.zeros((ROW_TILE, ROW_TILE), BF16)
    for k in range(TOP_K):
        hit = rid == jnp.broadcast_to(rel[k:k + 1, :], rid.shape)
        val = jnp.ones_like(out) if vals is None else jnp.broadcast_to(vals[k:k + 1, :], rid.shape)
        out = jnp.where(hit, val, out)
    return out


def _dispatch_kernel(dst_tab, tail_lo, tail_n,
                     x_ref, pos_ref, xs_hbm, loc, zbuf, sems):
    i = _flat_tile()
    n_tiles = pl.num_programs(0) * pl.num_programs(1)
    slot = i % 2
    pos = pos_ref[...]
    x = x_ref[0]
    cpb = ROW_TILE // CHUNK
    for rb in range(LOCAL_ROWS // ROW_TILE):
        p = _selection_block(pos, None, rb)
        loc[slot, rb * cpb:(rb + 1) * cpb] = jnp.dot(
            p, x, preferred_element_type=F32).astype(BF16).reshape(cpb, CHUNK, x.shape[1])
        for c in range(rb * cpb, (rb + 1) * cpb):
            _chunk_copy(loc.at[slot], c, xs_hbm, dst_tab[i * MAX_CHUNKS + c], sems.at[slot]).start()

    @pl.when(i > 0)
    def _():
        _wait_slot(loc.at[1 - slot], xs_hbm, sems.at[1 - slot], to_global=True)

    @pl.when(i == n_tiles - 1)
    def _():
        _wait_slot(loc.at[slot], xs_hbm, sems.at[slot], to_global=True)
        zbuf[...] = jnp.zeros_like(zbuf)

        def per_expert(e, n_started):
            def per_chunk(c, carry):
                _chunk_copy(zbuf, 0, xs_hbm, tail_lo[e] + c, sems.at[slot]).start()
                return carry

            lax.fori_loop(0, tail_n[e], per_chunk, 0)
            return n_started + tail_n[e]

        n_started = lax.fori_loop(0, N_EXPERTS, per_expert, 0)

        def drain(c, carry):
            _chunk_copy(zbuf, 0, xs_hbm, 0, sems.at[slot]).wait()
            return carry

        lax.fori_loop(0, n_started, drain, 0)


def _dispatch(u2, pos, tabs, n_row_tiles):
    n_batch, t, d = u2.shape
    tiles = t // ROW_TILE
    pair = pl.BlockSpec((TOP_K, ROW_TILE), lambda b, j, *_: (0, b * tiles + j))
    return pl.pallas_call(
        _dispatch_kernel,
        out_shape=jax.ShapeDtypeStruct((_buffer_chunks(n_row_tiles), CHUNK, d), BF16),
        grid_spec=pltpu.PrefetchScalarGridSpec(
            num_scalar_prefetch=3,
            grid=(n_batch, tiles),
            in_specs=[pl.BlockSpec((1, ROW_TILE, d), lambda b, j, *_: (b, j, 0)), pair],
            out_specs=pl.BlockSpec(memory_space=pl.ANY),
            scratch_shapes=[pltpu.VMEM((2, MAX_CHUNKS, CHUNK, d), BF16), pltpu.VMEM((1, CHUNK, d), BF16),
                            pltpu.SemaphoreType.DMA((2,))],
        ),
        compiler_params=_params(("arbitrary", "arbitrary")),
        name="dispatch",
    )(tabs["dst"], tabs["tail_lo"], tabs["tail_n"], u2, pos)


def _experts_kernel(te_tab, nv_tab, xs_ref, w1_ref, w3_ref, w2_ref, ys_ref, w1b, w3b, w2b):
    @pl.when(pl.program_id(0) < nv_tab[0])
    def _():
        j = pl.program_id(0)

        @pl.when((j == 0) | (te_tab[j] != te_tab[jnp.maximum(j - 1, 0)]))
        def _():
            w1b[...] = w1_ref[0, 0].astype(BF16)
            w3b[...] = w3_ref[0, 0].astype(BF16)
            w2b[...] = w2_ref[0, 0].astype(BF16)

        x = xs_ref[...]
        h1 = jnp.dot(x, w1b[...], preferred_element_type=F32)
        h3 = jnp.dot(x, w3b[...], preferred_element_type=F32)
        h = (_silu(h1) * h3).astype(BF16)
        ys_ref[...] = jnp.dot(h, w2b[...], preferred_element_type=F32).astype(BF16)


def _experts(xs, w1, w3, w2, layer, tile_expert, n_valid, n_row_tiles):
    rows, d = xs.shape
    de = w1.shape[3]
    row = pl.BlockSpec((EXPERT_ROW_TILE, d), lambda j, te, nv: (jnp.minimum(j, nv[0] - 1), 0))
    return pl.pallas_call(
        _experts_kernel,
        out_shape=jax.ShapeDtypeStruct((rows, d), BF16),
        grid_spec=pltpu.PrefetchScalarGridSpec(
            num_scalar_prefetch=2,
            grid=(n_row_tiles,),
            in_specs=[row,
                      pl.BlockSpec((1, 1, d, de), lambda j, te, nv: (layer, te[j], 0, 0)),
                      pl.BlockSpec((1, 1, d, de), lambda j, te, nv: (layer, te[j], 0, 0)),
                      pl.BlockSpec((1, 1, de, d), lambda j, te, nv: (layer, te[j], 0, 0))],
            out_specs=row,
            scratch_shapes=[pltpu.VMEM((d, de), BF16), pltpu.VMEM((d, de), BF16),
                            pltpu.VMEM((de, d), BF16)],
        ),
        compiler_params=_params(("arbitrary",)),
        name="experts",
    )(tile_expert, n_valid, xs, w1, w3, w2)


def _combine_kernel(src_tab,
                    ys_hbm, pos_ref, wts_ref, x1_ref, u2_ref, mod_ref, sw1_ref, sw3_ref, sw2_ref,
                    g_ref, b_ref, o_ref, loc, sems, *, n_batch, n_ctx_tiles, d, alpha):
    i = _flat_tile()
    n_tiles = pl.num_programs(0) * pl.num_programs(1)
    slot = i % COMBINE_SLOTS
    ahead = COMBINE_SLOTS - 1

    def fetch(tile, sl, chunks):
        for c in chunks:
            _chunk_copy(ys_hbm, src_tab[tile * MAX_CHUNKS + c], loc.at[sl], c, sems.at[sl]).start()

    @pl.when(i == 0)
    def _():
        for s in range(ahead):
            fetch(s % n_tiles, s, range(MAX_CHUNKS))

    _wait_slot(loc.at[slot], ys_hbm, sems.at[slot], to_global=False)

    nxt = (i + ahead) % n_tiles
    nxt_slot = (i + ahead) % COMBINE_SLOTS
    cpb = ROW_TILE // CHUNK
    pos = pos_ref[...]
    wts = wts_ref[...].astype(BF16)
    routed = jnp.zeros((ROW_TILE, d), F32)
    for rb in range(LOCAL_ROWS // ROW_TILE):
        rows = loc[slot, rb * cpb:(rb + 1) * cpb].reshape(ROW_TILE, d)
        sel_t = _selection_block(pos, wts, rb).T
        routed = routed + jnp.dot(sel_t, rows, preferred_element_type=F32)
        fetch(nxt, nxt_slot, range(rb * cpb, (rb + 1) * cpb))

    u = u2_ref[0]
    hs = (_silu(jnp.dot(u, sw1_ref[...], preferred_element_type=F32))
          * jnp.dot(u, sw3_ref[...], preferred_element_type=F32)).astype(BF16)
    shared = jnp.dot(hs, sw2_ref[...], preferred_element_type=F32)

    m = _mod_row(mod_ref, n_batch, n_ctx_tiles)
    g2 = m[:, 5 * d:6 * d]
    o_ref[0] = _layer_norm(alpha * x1_ref[0] + g2 * (routed + shared), g_ref[...], b_ref[...])

    @pl.when(i == n_tiles - 1)
    def _():
        for s in range(1, COMBINE_SLOTS):
            sl = (i + s) % COMBINE_SLOTS
            _wait_slot(loc.at[sl], ys_hbm, sems.at[sl], to_global=False)


def _combine(ys, pos, wts, x1, u2, mod_l, sw1, sw3, sw2, ln_g, ln_b, tabs, n_ctx, alpha, latent_only):
    n_batch, t, d = x1.shape
    tiles = t // ROW_TILE
    n_ctx_tiles = n_ctx // ROW_TILE
    row = pl.BlockSpec((1, ROW_TILE, d), lambda b, j, *_: (b, j, 0))
    pair = pl.BlockSpec((TOP_K, ROW_TILE), lambda b, j, *_: (0, b * tiles + j))
    out_rows, out_row = t, row
    if latent_only:
        out_rows = t - n_ctx
        out_row = pl.BlockSpec((1, ROW_TILE, d), lambda b, j, *_: (b, jnp.maximum(j - n_ctx_tiles, 0), 0))
    return pl.pallas_call(
        functools.partial(_combine_kernel, n_batch=n_batch, n_ctx_tiles=n_ctx_tiles, d=d,
                          alpha=alpha),
        out_shape=jax.ShapeDtypeStruct((n_batch, out_rows, d), F32),
        grid_spec=pltpu.PrefetchScalarGridSpec(
            num_scalar_prefetch=1,
            grid=(n_batch, tiles),
            in_specs=[pl.BlockSpec(memory_space=pl.ANY), pair, pair,
                      row, row] + [_const_spec(a.shape) for a in (mod_l, sw1, sw3, sw2, ln_g, ln_b)],
            out_specs=out_row,
            scratch_shapes=[pltpu.VMEM((COMBINE_SLOTS, MAX_CHUNKS, CHUNK, d), BF16),
                            pltpu.SemaphoreType.DMA((COMBINE_SLOTS,))],
        ),
        compiler_params=_params(("arbitrary", "arbitrary")),
        name="combine",
    )(tabs["src"], ys, pos, wts, x1, u2, mod_l, sw1, sw3, sw2, ln_g, ln_b)


def _dispatch_tables(nch, lo, n_row_tiles):
    cpt = EXPERT_ROW_TILE // CHUNK
    tot_e = jnp.sum(nch, axis=0)
    region = (tot_e + cpt - 1) // cpt * cpt
    ends = jnp.cumsum(region)
    base = ends - region
    g = base[None, :] + jnp.cumsum(nch, axis=0) - nch
    c = jnp.arange(MAX_CHUNKS, dtype=nch.dtype)[None, :, None]
    covers = (lo[:, None, :] <= c) & (c < (lo + nch)[:, None, :])
    used = jnp.any(covers, axis=-1)
    dst = c[:, :, 0] + jnp.sum(jnp.where(covers, (g - lo)[:, None, :], 0), axis=-1)
    slot = (jnp.arange(nch.shape[0], dtype=nch.dtype) % 2)[:, None]
    spare = n_row_tiles * cpt + slot * MAX_CHUNKS + c[:, :, 0]
    row_tile = jnp.arange(n_row_tiles, dtype=nch.dtype)[:, None]
    tile_expert = jnp.sum((ends // cpt)[None, :] <= row_tile, axis=-1)
    i32 = lambda a: a.astype(jnp.int32)
    tabs = dict(dst=i32(jnp.where(used, dst, spare).reshape(-1)),
                src=i32(jnp.where(used, dst, 0).reshape(-1)),
                tail_lo=i32(base + tot_e), tail_n=i32(region - tot_e))
    return tabs, i32(jnp.minimum(tile_expert, N_EXPERTS - 1)), i32(ends[-1:] // cpt)


def _buffer_chunks(n_row_tiles):
    return n_row_tiles * (EXPERT_ROW_TILE // CHUNK) + 2 * MAX_CHUNKS


def _rope_tables(n_ctx, n_lat):
    pos = jnp.arange(n_lat)
    n_freq = HEAD_DIM // 4
    inv = ROPE_THETA ** (-jnp.arange(n_freq, dtype=F32) / n_freq)
    ang_r = (pos // GRID_W).astype(F32)[:, None] * inv
    ang_c = (pos % GRID_W).astype(F32)[:, None] * inv
    cos_l = jnp.concatenate([jnp.cos(ang_r)] * 2 + [jnp.cos(ang_c)] * 2, axis=1)
    sin_l = jnp.concatenate([-jnp.sin(ang_r), jnp.sin(ang_r), -jnp.sin(ang_c), jnp.sin(ang_c)], axis=1)
    cos_t = jnp.concatenate([jnp.ones((n_ctx, HEAD_DIM), F32), cos_l], axis=0)
    sin_t = jnp.concatenate([jnp.zeros((n_ctx, HEAD_DIM), F32), sin_l], axis=0)
    return cos_t, sin_t


def _pack_gate_weights(wa, wi):
    n_dir, n_blocks, w, _ = wa.shape
    per_tile = MXU_TILE // w
    n_tiles = n_blocks // per_tile

    def dense(wx):
        wx = wx.reshape(n_dir, n_tiles, per_tile, w, w)
        eye = jnp.eye(per_tile, dtype=wx.dtype)
        full = jnp.einsum('dtpij,pq->dtpiqj', wx, eye)
        return full.reshape(n_dir, n_tiles, MXU_TILE, MXU_TILE)

    return jnp.concatenate([dense(wa), dense(wi)], axis=-1).astype(BF16)


def kernel(x, c, ctx, c_ctx, w_mod, b_mod, w_in, sink, conv_w, conv_b, rg_wa, rg_ba, rg_wi, rg_bi,
           rg_lambda, w_o_attn, w_o_rnn, w_out, ln1_g, ln1_b, router_w, router_bias, exp_w1, exp_w3,
           exp_w2, sh_w1, sh_w3, sh_w2, ln2_g, ln2_b):
    n_batch, n_lat, d = x.shape
    n_ctx = ctx.shape[1]
    n_layers = w_mod.shape[0]
    t = n_ctx + n_lat
    assert n_ctx % ROW_TILE == 0 and n_lat % ROW_TILE == 0 and n_batch + 1 <= MOD_ROWS
    assert exp_w1.shape[1:] == (N_EXPERTS, d, D_EXPERT) and sh_w1.shape[1:] == (d, D_EXPERT)
    alpha = (2 * n_layers) ** 0.25
    n_tiles = n_batch * t // ROW_TILE
    chunks_per_row_tile = EXPERT_ROW_TILE // CHUNK
    max_chunks = (TOP_K * n_batch * t // CHUNK + n_tiles * N_EXPERTS
                  + N_EXPERTS * (chunks_per_row_tile - 1))
    n_row_tiles = pl.cdiv(max_chunks, chunks_per_row_tile)

    xa = jnp.concatenate([ctx, x], axis=1)
    cc = jnp.zeros((MOD_ROWS, d), F32).at[:n_batch].set(c).at[n_batch].set(c_ctx)
    mod = _modulation(cc, w_mod, b_mod)
    cos_t, sin_t = _rope_tables(n_ctx, n_lat)
    mask_bias = _attn_mask_bias(n_ctx)

    for l in range(n_layers):
        mod_l = mod[l]
        q, k, v, xr, gy, sa, sr = _in_projection(xa, mod_l, w_in[l].astype(BF16), cos_t, sin_t, n_ctx)
        attn = _attention(sink[l], q, k, v, mask_bias, n_ctx)
        wg = _pack_gate_weights(rg_wa[l], rg_wi[l])
        hf, hb = _rnn_branch(xr, conv_w[l], conv_b[l][None, :], wg, rg_ba[l], rg_bi[l], rg_lambda[l],
                             n_ctx)
        rw_t = router_w[l].T
        rw_hi = rw_t.astype(BF16)
        rw_lo = (rw_t - rw_hi.astype(F32)).astype(BF16)
        x1, u2, logits_t = _out_projection(
            xa, attn, hf, hb, gy, sa, sr, mod_l, w_o_attn[l].astype(BF16), w_o_rnn[l].astype(BF16),
            w_out[l].astype(BF16), ln1_g[l][None, :], ln1_b[l][None, :], rw_hi, rw_lo, n_ctx, alpha)
        pos, wts, nch, lo = _route(logits_t, router_bias[l][:, None])
        tabs, tile_expert, n_valid = _dispatch_tables(nch[:, :, 0], lo[:, :, 0], n_row_tiles)
        xs = _dispatch(u2, pos, tabs, n_row_tiles)
        ys = _experts(xs.reshape(-1, d), exp_w1, exp_w3, exp_w2, l, tile_expert, n_valid, n_row_tiles)
        xa = _combine(ys.reshape(xs.shape), pos, wts, x1, u2, mod_l, sh_w1[l].astype(BF16), sh_w3[l].astype(BF16),
                      sh_w2[l].astype(BF16), ln2_g[l][None, :], ln2_b[l][None, :], tabs, n_ctx, alpha,
                      latent_only=(l == n_layers - 1))
    return xa
```

```python
import functools
import math

import jax
import jax.numpy as jnp
from jax import lax
from jax.experimental import pallas as pl
from jax.experimental.pallas import tpu as pltpu

N_HEADS = 8
N_KV_HEADS = 2
HEAD_DIM = 128
KV_GROUP = N_HEADS // N_KV_HEADS
ATT_BLOCK = 128
GRID_W = 64
ROPE_THETA = 10000.0
RNN_BLOCKS = 16
LRU_C = 8.0
N_EXPERTS = 64
TOP_K = 8
N_GROUPS = 8
TOPK_GROUPS = 4
D_EXPERT = 256
ROUTE_SCALE = 2.5
LN_EPS = 1e-6

ROW_TILE = 256
MXU_TILE = 256
CHUNK = 16
LOCAL_ROWS = TOP_K * ROW_TILE + N_EXPERTS * CHUNK
MAX_CHUNKS = LOCAL_ROWS // CHUNK
STATIC_BLOCKS = 10
STATIC_CHUNKS = STATIC_BLOCKS * ROW_TILE // CHUNK
EXPERT_ROW_TILE = 1024
COMBINE_SLOTS = 3
MOD_ROWS = 24
VMEM_LIMIT = 56 * 1024 * 1024

F32 = jnp.float32
BF16 = jnp.bfloat16
NEG_BIG = -1e30


def _const_spec(shape):
    zeros = (0,) * len(shape)
    return pl.BlockSpec(shape, lambda *_: zeros, pipeline_mode=pl.Buffered(1))


def _params(sem):
    return pltpu.CompilerParams(dimension_semantics=sem, vmem_limit_bytes=VMEM_LIMIT)


def _silu(v):
    return v * jax.nn.sigmoid(v)


def _gelu_tanh(v):
    return v * (0.5 * (1.0 + jnp.tanh(math.sqrt(2.0 / math.pi) * (v + 0.044715 * (v * v * v)))))


def _layer_norm(v, g, b):
    mu = jnp.mean(v, axis=-1, keepdims=True)
    d = v - mu
    var = jnp.mean(d * d, axis=-1, keepdims=True)
    return d * lax.rsqrt(var + LN_EPS) * g + b


def _mod_kernel(cc_ref, w_ref, b_ref, o_ref):
    s = _silu(cc_ref[...]).astype(BF16)
    o_ref[0] = jnp.dot(s, w_ref[0].astype(BF16), preferred_element_type=F32) + b_ref[0]


def _modulation(cc, w_mod, b_mod):
    n_layers, d, d6 = w_mod.shape
    col = 1536
    return pl.pallas_call(
        _mod_kernel,
        out_shape=jax.ShapeDtypeStruct((n_layers, MOD_ROWS, d6), F32),
        grid=(n_layers, d6 // col),
        in_specs=[
            pl.BlockSpec((MOD_ROWS, d), lambda l, j: (0, 0)),
            pl.BlockSpec((1, d, col), lambda l, j: (l, 0, j)),
            pl.BlockSpec((1, 1, col), lambda l, j: (l, 0, j)),
        ],
        out_specs=pl.BlockSpec((1, MOD_ROWS, col), lambda l, j: (l, 0, j)),
        compiler_params=_params(("parallel", "parallel")),
        name="modulation",
    )(cc, w_mod, b_mod.reshape(n_layers, 1, d6))


def _tiles_per_step(tiles):
    return next(s for s in (3, 2, 1) if tiles % s == 0)


def _mod_row(mod_ref, n_batch, n_ctx_tiles):
    b = pl.program_id(0)
    j = pl.program_id(1)
    row = jnp.where(j < n_ctx_tiles, n_batch, b)
    return mod_ref[pl.ds(row, 1), :]


def _rope(v, cos_w, sin_w):
    width = v.shape[1]
    lane = lax.broadcasted_iota(jnp.int32, v.shape, 1)
    partner = jnp.where((lane & 32) == 0, pltpu.roll(v, width - 32, 1), pltpu.roll(v, 32, 1))
    return v * cos_w + partner * sin_w


def _inproj_kernel(x_ref, mod_ref, w_ref, cos_ref, sin_ref,
                   q_ref, k_ref, v_ref, xr_ref, gy_ref, sa_ref, sr_ref,
                   *, n_batch, n_ctx_tiles, d, sub):
    b = pl.program_id(0)
    j = pl.program_id(1)
    parts = []
    for s in range(sub):
        mrow = jnp.where(j * sub + s < n_ctx_tiles, n_batch, b)
        m = mod_ref[pl.ds(mrow, 1), :]
        xs = x_ref[0, s * ROW_TILE:(s + 1) * ROW_TILE, :]
        parts.append((xs * (1.0 + m[:, d:2 * d]) + m[:, 0:d]).astype(BF16))
    u = jnp.concatenate(parts, axis=0)

    def proj(lo, hi):
        return jnp.dot(u, w_ref[:, lo:hi], preferred_element_type=F32)

    qw = N_HEADS * HEAD_DIM
    kw = N_KV_HEADS * HEAD_DIM
    cos1 = cos_ref[...]
    sin1 = sin_ref[...]
    o = 0
    q = proj(o, o + qw) * (HEAD_DIM ** -0.5)
    q_ref[0] = _rope(q, jnp.concatenate([cos1] * N_HEADS, axis=1),
                     jnp.concatenate([sin1] * N_HEADS, axis=1)).astype(BF16)
    o += qw
    k = proj(o, o + kw)
    k_ref[0] = _rope(k, jnp.concatenate([cos1] * N_KV_HEADS, axis=1),
                     jnp.concatenate([sin1] * N_KV_HEADS, axis=1)).astype(BF16)
    o += kw
    v_ref[0] = proj(o, o + kw).astype(BF16)
    o += kw
    xr_ref[0] = proj(o, o + d)
    o += d
    gy_ref[0] = _gelu_tanh(proj(o, o + d)).astype(BF16)
    o += d
    sa_ref[0] = jax.nn.sigmoid(proj(o, o + d)).astype(BF16)
    o += d
    sr_ref[0] = jax.nn.sigmoid(proj(o, o + d)).astype(BF16)


def _in_projection(xa, mod_l, w_in, cos_t, sin_t, n_ctx):
    n_batch, t, d = xa.shape
    qw = N_HEADS * HEAD_DIM
    kw = N_KV_HEADS * HEAD_DIM
    tiles = t // ROW_TILE
    sub = _tiles_per_step(tiles)
    row = lambda w: pl.BlockSpec((1, sub * ROW_TILE, w), lambda b, j: (b, j, 0))
    tab = pl.BlockSpec((sub * ROW_TILE, HEAD_DIM), lambda b, j: (j, 0))
    shp = lambda w, dt: jax.ShapeDtypeStruct((n_batch, t, w), dt)
    return pl.pallas_call(
        functools.partial(_inproj_kernel, n_batch=n_batch, n_ctx_tiles=n_ctx // ROW_TILE, d=d,
                          sub=sub),
        out_shape=(shp(qw, BF16), shp(kw, BF16), shp(kw, BF16), shp(d, F32),
                   shp(d, BF16), shp(d, BF16), shp(d, BF16)),
        grid=(n_batch, tiles // sub),
        in_specs=[row(d), _const_spec(mod_l.shape), _const_spec(w_in.shape), tab, tab],
        out_specs=(row(qw), row(kw), row(kw), row(d), row(d), row(d), row(d)),
        compiler_params=_params(("parallel", "parallel")),
        name="in_projection",
    )(xa, mod_l, w_in, cos_t, sin_t)


def _attn_mask_bias(n_ctx):
    n_win = 3 * ATT_BLOCK
    rr = jnp.arange(ATT_BLOCK)[:, None]
    jj = jnp.arange(n_win + n_ctx)[None, :]
    in_band = (jj >= rr) & (jj <= rr + 2 * ATT_BLOCK)
    is_ctx = jj >= n_win
    kinds = [is_ctx | jnp.zeros_like(in_band)]
    for has_left, has_right in ((False, False), (True, False), (False, True), (True, True)):
        ok = in_band & (has_left | (jj >= ATT_BLOCK)) & (has_right | (jj < 2 * ATT_BLOCK))
        kinds.append(is_ctx | ok)
    return jnp.where(jnp.stack(kinds), 0.0, NEG_BIG).astype(F32)


def _attn_kernel(sink_ref, q_ref, kp_ref, kc_ref, kn_ref, kx_ref, vp_ref, vc_ref, vn_ref, vx_ref,
                 bias_ref, o_ref, *, n_ctx_blocks, n_lat_blocks):
    i = pl.program_id(1) - n_ctx_blocks
    kind = jnp.where(i < 0, 0, 1 + (i > 0).astype(jnp.int32) + 2 * (i < n_lat_blocks - 1).astype(jnp.int32))
    rows = KV_GROUP * ATT_BLOCK
    bias1 = bias_ref[kind]
    bias = jnp.concatenate([bias1] * KV_GROUP, axis=0)
    rblk = lax.broadcasted_iota(jnp.int32, (rows, 1), 0) // ATT_BLOCK
    q = q_ref[0]
    for h in range(N_KV_HEADS):
        hs = slice(h * HEAD_DIM, (h + 1) * HEAD_DIM)
        kk = jnp.concatenate([kp_ref[0, :, hs], kc_ref[0, :, hs], kn_ref[0, :, hs], kx_ref[0, :, hs]], axis=0)
        vv = jnp.concatenate([vp_ref[0, :, hs], vc_ref[0, :, hs], vn_ref[0, :, hs], vx_ref[0, :, hs]], axis=0)
        heads = [h * KV_GROUP + g for g in range(KV_GROUP)]
        qs = jnp.concatenate([q[:, hd * HEAD_DIM:(hd + 1) * HEAD_DIM] for hd in heads], axis=0)
        s = lax.dot_general(qs, kk, (((1,), (1,)), ((), ())), preferred_element_type=F32) + bias
        sink = jnp.zeros((rows, 1), F32)
        for g, hd in enumerate(heads):
            sink = jnp.where(rblk == g, sink_ref[hd], sink)
        mx = jnp.maximum(jnp.max(s, axis=1, keepdims=True), sink)
        p = jnp.exp(s - mx)
        den = jnp.sum(p, axis=1, keepdims=True) + jnp.exp(sink - mx)
        o = jnp.dot(p.astype(BF16), vv, preferred_element_type=F32) / den
        for g, hd in enumerate(heads):
            o_ref[0, :, hd * HEAD_DIM:(hd + 1) * HEAD_DIM] = o[g * ATT_BLOCK:(g + 1) * ATT_BLOCK].astype(BF16)


def _attention(sink_l, q, k, v, mask_bias, n_ctx):
    n_batch, t, qw = q.shape
    kw = k.shape[2]
    ncb = n_ctx // ATT_BLOCK
    nlb = (t - n_ctx) // ATT_BLOCK

    def lat_blk(off):
        def index(b, j, sink):
            return (b, ncb + jnp.clip(j - ncb + off, 0, nlb - 1), 0)
        return pl.BlockSpec((1, ATT_BLOCK, kw), index)

    ctx_blk = pl.BlockSpec((1, n_ctx, kw), lambda b, j, sink: (b, 0, 0))
    q_blk = pl.BlockSpec((1, ATT_BLOCK, qw), lambda b, j, sink: (b, j, 0))
    kv_specs = [lat_blk(-1), lat_blk(0), lat_blk(1), ctx_blk]
    return pl.pallas_call(
        functools.partial(_attn_kernel, n_ctx_blocks=ncb, n_lat_blocks=nlb),
        out_shape=jax.ShapeDtypeStruct((n_batch, t, qw), BF16),
        grid_spec=pltpu.PrefetchScalarGridSpec(
            num_scalar_prefetch=1,
            grid=(n_batch, t // ATT_BLOCK),
            in_specs=[q_blk] + kv_specs + kv_specs + [_const_spec(mask_bias.shape)],
            out_specs=q_blk,
        ),
        compiler_params=_params(("parallel", "parallel")),
        name="attention",
    )(sink_l, q, k, k, k, k, v, v, v, v, mask_bias)


def _rnn_kernel(x_ref, cw_ref, cb_ref, wg_ref, ba_ref, bi_ref, lam_ref, hf_ref, hb_ref,
                a_scr, b_scr, h_scr, *, n_ctx, n_rows):
    j = pl.program_id(1)
    tb = ROW_TILE
    n_blk = n_rows // tb
    n_cblk = n_ctx // tb
    blk_f = j
    blk_b = jnp.where(j < n_cblk, n_cblk - 1 - j, n_blk - 1 - (j - n_cblk))
    d = x_ref.shape[2]
    ridx = lax.broadcasted_iota(jnp.int32, (tb, d), 0)

    def conv_block(blk):
        t0 = pl.multiple_of(blk * tb, tb)
        cur = x_ref[0, pl.ds(t0, tb), :]
        prev8 = x_ref[0, pl.ds(pl.multiple_of(jnp.maximum(t0 - 8, 0), 8), 8), :]
        next8 = x_ref[0, pl.ds(pl.multiple_of(jnp.minimum(t0 + tb, n_rows - 8), 8), 8), :]
        prev_ok = (t0 != 0) & (t0 != n_ctx)
        next_ok = (t0 + tb != n_ctx) & (t0 + tb != n_rows)
        prev8 = jnp.where(prev_ok, prev8, 0.0)
        next8 = jnp.where(next_ok, next8, 0.0)
        p6 = jnp.broadcast_to(prev8[6:7, :], (tb, d))
        p7 = jnp.broadcast_to(prev8[7:8, :], (tb, d))
        n0 = jnp.broadcast_to(next8[0:1, :], (tb, d))
        xm1 = jnp.where(ridx == 0, p7, pltpu.roll(cur, 1, 0))
        xm2 = jnp.where(ridx == 0, p6, jnp.where(ridx == 1, p7, pltpu.roll(cur, 2, 0)))
        xp1 = jnp.where(ridx == tb - 1, n0, pltpu.roll(cur, tb - 1, 0))
        return (cb_ref[...] + xm2 * cw_ref[0:1, :] + xm1 * cw_ref[1:2, :]
                + cur * cw_ref[2:3, :] + xp1 * cw_ref[3:4, :])

    def gate_block(xc, dr):
        xb = xc.astype(BF16)
        r_parts, i_parts = [], []
        for jt in range(d // MXU_TILE):
            g = jnp.dot(xb[:, jt * MXU_TILE:(jt + 1) * MXU_TILE], wg_ref[dr, jt],
                        preferred_element_type=F32)
            r_parts.append(g[:, :MXU_TILE])
            i_parts.append(g[:, MXU_TILE:])
        r = jax.nn.sigmoid(jnp.concatenate(r_parts, axis=1) + ba_ref[dr:dr + 1, :])
        ig = jax.nn.sigmoid(jnp.concatenate(i_parts, axis=1) + bi_ref[dr:dr + 1, :])
        nl = -lam_ref[dr:dr + 1, :]
        softplus = jnp.maximum(nl, 0.0) + jnp.log1p(jnp.exp(-jnp.abs(nl)))
        log_a = (-LRU_C) * r * softplus
        a = jnp.exp(log_a)
        a_scr[dr] = a
        gap = 1.0 - a * a
        root = jnp.where(gap > 0.0, gap * lax.rsqrt(gap), 0.0)
        b_scr[dr] = root * (ig * xc)

    gate_block(conv_block(blk_f), 0)
    gate_block(conv_block(blk_b), 1)

    @pl.when(j == 0)
    def _():
        h_scr[...] = jnp.zeros_like(h_scr)

    def step(s, carry):
        hf, hb = carry
        tf = s
        tr = tb - 1 - s
        hf = a_scr[0, pl.ds(tf, 1), :] * hf + b_scr[0, pl.ds(tf, 1), :]
        hb = a_scr[1, pl.ds(tr, 1), :] * hb + b_scr[1, pl.ds(tr, 1), :]
        hf_ref[0, pl.ds(tf, 1), :] = hf
        hb_ref[0, pl.ds(tr, 1), :] = hb
        return hf, hb

    hf, hb = lax.fori_loop(0, tb, step, (h_scr[0:1, :], h_scr[1:2, :]), unroll=8)
    h_scr[0:1, :] = hf
    h_scr[1:2, :] = hb


def _rnn_branch(xr, conv_w, conv_b, wg, ba, bi, lam, n_ctx):
    n_batch, t, d = xr.shape
    n_blk = t // ROW_TILE
    n_cblk = n_ctx // ROW_TILE

    def bwd_index(b, j):
        return (b, jnp.where(j < n_cblk, n_cblk - 1 - j, n_blk - 1 - (j - n_cblk)), 0)

    out = jax.ShapeDtypeStruct((n_batch, t, d), F32)
    return pl.pallas_call(
        functools.partial(_rnn_kernel, n_ctx=n_ctx, n_rows=t),
        out_shape=(out, out),
        grid=(n_batch, n_blk),
        in_specs=[
            pl.BlockSpec((1, t, d), lambda b, j: (b, 0, 0)),
            _const_spec(conv_w.shape), _const_spec(conv_b.shape), _const_spec(wg.shape),
            _const_spec(ba.shape), _const_spec(bi.shape), _const_spec(lam.shape),
        ],
        out_specs=(pl.BlockSpec((1, ROW_TILE, d), lambda b, j: (b, j, 0)),
                   pl.BlockSpec((1, ROW_TILE, d), bwd_index)),
        scratch_shapes=[pltpu.VMEM((2, ROW_TILE, d), F32), pltpu.VMEM((2, ROW_TILE, d), F32),
                        pltpu.VMEM((8, d), F32)],
        compiler_params=_params(("parallel", "arbitrary")),
        name="rnn_branch",
    )(xr, conv_w, conv_b, wg, ba, bi, lam)


def _dot_nt(a, b):
    return lax.dot_general(a, b, (((1,), (1,)), ((), ())), preferred_element_type=F32)


def _outproj_kernel(x_ref, attn_ref, hf_ref, hb_ref, gy_ref, sa_ref, sr_ref, mod_ref,
                    woa_ref, wor_ref, wout_ref, g_ref, b_ref, rwh_ref, rwl_ref,
                    x1_ref, u2_ref, lg_ref, *, n_batch, n_ctx_tiles, d, alpha, sub):
    b = pl.program_id(0)
    j = pl.program_id(1)
    for s in range(sub):
        rs = slice(s * ROW_TILE, (s + 1) * ROW_TILE)
        mrow = jnp.where(j * sub + s < n_ctx_tiles, n_batch, b)
        m = mod_ref[pl.ds(mrow, 1), :]
        g1 = m[:, 2 * d:3 * d]
        sh2 = m[:, 3 * d:4 * d]
        sc2 = m[:, 4 * d:5 * d]
        a = jnp.dot(attn_ref[0, rs, :], woa_ref[...], preferred_element_type=F32)
        rnn = ((hf_ref[0, rs, :] + hb_ref[0, rs, :]) * gy_ref[0, rs, :].astype(F32)).astype(BF16)
        r = jnp.dot(rnn, wor_ref[...], preferred_element_type=F32)
        merged = (sa_ref[0, rs, :].astype(F32) * a + sr_ref[0, rs, :].astype(F32) * r).astype(BF16)
        mix = jnp.dot(merged, wout_ref[...], preferred_element_type=F32)
        x1 = _layer_norm(alpha * x_ref[0, rs, :] + g1 * mix, g_ref[...], b_ref[...])
        x1_ref[0, rs, :] = x1
        u2 = x1 * (1.0 + sc2) + sh2
        u_hi = u2.astype(BF16)
        u2_ref[0, rs, :] = u_hi
        u_lo = (u2 - u_hi.astype(F32)).astype(BF16)
        lg_ref[:, rs] = (_dot_nt(rwh_ref[...], u_hi) + _dot_nt(rwh_ref[...], u_lo)
                         + _dot_nt(rwl_ref[...], u_hi))


def _out_projection(xa, attn, hf, hb, gy, sa, sr, mod_l, woa, wor, wout, ln_g, ln_b, rwh, rwl,
                    n_ctx, alpha):
    n_batch, t, d = xa.shape
    tiles = t // ROW_TILE
    sub = _tiles_per_step(tiles)
    steps = tiles // sub
    row = pl.BlockSpec((1, sub * ROW_TILE, d), lambda b, j: (b, j, 0))
    return pl.pallas_call(
        functools.partial(_outproj_kernel, n_batch=n_batch, n_ctx_tiles=n_ctx // ROW_TILE, d=d,
                          alpha=alpha, sub=sub),
        out_shape=(jax.ShapeDtypeStruct((n_batch, t, d), F32),
                   jax.ShapeDtypeStruct((n_batch, t, d), BF16),
                   jax.ShapeDtypeStruct((N_EXPERTS, n_batch * t), F32)),
        grid=(n_batch, steps),
        in_specs=[row] * 7 + [_const_spec(a.shape) for a in
                              (mod_l, woa, wor, wout, ln_g, ln_b, rwh, rwl)],
        out_specs=(row, row,
                   pl.BlockSpec((N_EXPERTS, sub * ROW_TILE), lambda b, j: (0, b * steps + j))),
        compiler_params=_params(("parallel", "parallel")),
        name="out_projection",
    )(xa, attn, hf, hb, gy, sa, sr, mod_l, woa, wor, wout, ln_g, ln_b, rwh, rwl)


def _route_kernel(lg_ref, bias_ref, pos_ref, wts_ref, nch_ref, lo_ref):
    lg = lg_ref[...]
    n_tok = lg.shape[1]
    per_group = N_EXPERTS // N_GROUPS
    scores = jax.nn.sigmoid(lg)
    biased = scores + bias_ref[...]
    b3 = biased.reshape(N_GROUPS, per_group, n_tok)
    sub = lax.broadcasted_iota(jnp.int32, b3.shape, 1)
    m1 = jnp.max(b3, axis=1, keepdims=True)
    i1 = jnp.min(jnp.where(b3 == m1, sub, per_group), axis=1, keepdims=True)
    m2 = jnp.max(jnp.where(sub == i1, -jnp.inf, b3), axis=1, keepdims=True)
    grp = (m1 + m2).reshape(N_GROUPS, n_tok)

    gi = lax.broadcasted_iota(jnp.int32, grp.shape, 0)
    gsel = jnp.zeros(grp.shape, F32)
    for _ in range(TOPK_GROUPS):
        m = jnp.max(grp, axis=0, keepdims=True)
        idx = jnp.min(jnp.where(grp == m, gi, N_GROUPS), axis=0, keepdims=True)
        hit = gi == idx
        gsel = jnp.where(hit, 1.0, gsel)
        grp = jnp.where(hit, -jnp.inf, grp)
    emask = jnp.broadcast_to(gsel.reshape(N_GROUPS, 1, n_tok), b3.shape).reshape(N_EXPERTS, n_tok)
    cand = jnp.where(emask > 0.0, biased, -jnp.inf)

    ei = lax.broadcasted_iota(jnp.int32, cand.shape, 0)
    comb = jnp.zeros(cand.shape, F32)
    picked = jnp.zeros(cand.shape, F32)
    hits = []
    for _ in range(TOP_K):
        m = jnp.max(cand, axis=0, keepdims=True)
        idx = jnp.min(jnp.where(cand == m, ei, N_EXPERTS), axis=0, keepdims=True)
        hit = ei == idx
        hits.append(hit)
        comb = jnp.where(hit, scores, comb)
        picked = jnp.where(hit, 1.0, picked)
        cand = jnp.where(hit, -jnp.inf, cand)
    comb = comb / jnp.sum(comb, axis=0, keepdims=True) * ROUTE_SCALE

    cnt = jnp.sum(picked, axis=1, keepdims=True)
    nch = jnp.broadcast_to(jnp.floor((cnt + (CHUNK - 1)) * (1.0 / CHUNK)), (N_EXPERTS, 128))
    erow = lax.broadcasted_iota(jnp.int32, nch.shape, 0)
    incl = nch
    for s in (1, 2, 4, 8, 16, 32):
        incl = incl + jnp.where(erow >= s, pltpu.roll(incl, s, 0), 0.0)
    lo = incl - nch
    earlier = (lax.broadcasted_iota(jnp.int32, (n_tok, n_tok), 0)
               < lax.broadcasted_iota(jnp.int32, (n_tok, n_tok), 1)).astype(BF16)
    rank = jnp.dot(picked.astype(BF16), earlier, preferred_element_type=F32)
    slot = lo[:, 0:1] * CHUNK + rank
    pos = jnp.concatenate([jnp.sum(jnp.where(h, slot, 0.0), axis=0, keepdims=True) for h in hits], axis=0)
    wts = jnp.concatenate([jnp.sum(jnp.where(h, comb, 0.0), axis=0, keepdims=True) for h in hits], axis=0)
    pos_ref[...] = pos.astype(jnp.int32)
    wts_ref[...] = wts
    nch_ref[0] = nch.astype(jnp.int32)
    lo_ref[0] = lo.astype(jnp.int32)


def _route(logits_t, bias_col):
    n_tok = logits_t.shape[1]
    n_tiles = n_tok // ROW_TILE
    tab = jax.ShapeDtypeStruct((n_tiles, N_EXPERTS, 128), jnp.int32)
    tab_spec = pl.BlockSpec((1, N_EXPERTS, 128), lambda i: (i, 0, 0))
    pair_spec = pl.BlockSpec((TOP_K, ROW_TILE), lambda i: (0, i))
    return pl.pallas_call(
        _route_kernel,
        out_shape=(jax.ShapeDtypeStruct((TOP_K, n_tok), jnp.int32),
                   jax.ShapeDtypeStruct((TOP_K, n_tok), F32), tab, tab),
        grid=(n_tiles,),
        in_specs=[pl.BlockSpec((N_EXPERTS, ROW_TILE), lambda i: (0, i)), _const_spec(bias_col.shape)],
        out_specs=(pair_spec, pair_spec, tab_spec, tab_spec),
        compiler_params=_params(("parallel",)),
        name="route",
    )(logits_t, bias_col)


def _chunk_copy(src_ref, src_chunk, dst_ref, dst_chunk, sem):
    return pltpu.make_async_copy(src_ref.at[src_chunk], dst_ref.at[dst_chunk], sem)


def _flat_tile():
    return pl.program_id(0) * pl.num_programs(1) + pl.program_id(1)


def _wait_slot(local_slot_ref, global_ref, sem, to_global, tail):
    first, count = (STATIC_CHUNKS, MAX_CHUNKS - STATIC_CHUNKS) if tail else (0, STATIC_CHUNKS)
    local = local_slot_ref.at[pl.ds(first, count)]
    remote = global_ref.at[pl.ds(0, count)]
    if to_global:
        pltpu.make_async_copy(local, remote, sem).wait()
    else:
        pltpu.make_async_copy(remote, local, sem).wait()


def _wait_tile(tail_on, tile, local_slot_ref, global_ref, sem, to_global):
    _wait_slot(local_slot_ref, global_ref, sem, to_global, tail=False)

    @pl.when(tail_on[tile] != 0)
    def _():
        _wait_slot(local_slot_ref, global_ref, sem, to_global, tail=True)


def _selection_block(pos, vals, rb):
    rid = lax.broadcasted_iota(jnp.int32, (ROW_TILE, ROW_TILE), 0).astype(F32).astype(BF16)
    rel = pos - rb * ROW_TILE
    rel = jnp.where(rel < 0, -1, jnp.where(rel >= ROW_TILE, -1, rel)).astype(F32).astype(BF16)
    out = jnp.zeros((ROW_TILE, ROW_TILE), BF16)
    for k in range(TOP_K):
        hit = rid == jnp.broadcast_to(rel[k:k + 1, :], rid.shape)
        val = jnp.ones_like(out) if vals is None else jnp.broadcast_to(vals[k:k + 1, :], rid.shape)
        out = jnp.where(hit, val, out)
    return out


def _dispatch_kernel(dst_tab, tail_on, tail_lo, tail_n,
                     x_ref, pos_ref, xs_hbm, loc, zbuf, sems):
    i = _flat_tile()
    n_tiles = pl.num_programs(0) * pl.num_programs(1)
    slot = i % 2
    cpb = ROW_TILE // CHUNK

    def do_blocks(blocks):
        pos = pos_ref[...]
        x = x_ref[0]
        for rb in blocks:
            p = _selection_block(pos, None, rb)
            loc[slot, rb * cpb:(rb + 1) * cpb] = jnp.dot(
                p, x, preferred_element_type=F32).astype(BF16).reshape(cpb, CHUNK, x.shape[1])
            for c in range(rb * cpb, (rb + 1) * cpb):
                _chunk_copy(loc.at[slot], c, xs_hbm, dst_tab[i * MAX_CHUNKS + c], sems.at[slot]).start()

    do_blocks(range(STATIC_BLOCKS))

    @pl.when(tail_on[i] != 0)
    def _():
        do_blocks(range(STATIC_BLOCKS, LOCAL_ROWS // ROW_TILE))

    @pl.when(i > 0)
    def _():
        _wait_tile(tail_on, jnp.maximum(i - 1, 0), loc.at[1 - slot], xs_hbm, sems.at[1 - slot],
                   to_global=True)

    @pl.when(i == n_tiles - 1)
    def _():
        _wait_tile(tail_on, i, loc.at[slot], xs_hbm, sems.at[slot], to_global=True)
        zbuf[...] = jnp.zeros_like(zbuf)

        def per_expert(e, n_started):
            def per_chunk(c, carry):
                _chunk_copy(zbuf, 0, xs_hbm, tail_lo[e] + c, sems.at[slot]).start()
                return carry

            lax.fori_loop(0, tail_n[e], per_chunk, 0)
            return n_started + tail_n[e]

        n_started = lax.fori_loop(0, N_EXPERTS, per_expert, 0)

        def drain(c, carry):
            _chunk_copy(zbuf, 0, xs_hbm, 0, sems.at[slot]).wait()
            return carry

        lax.fori_loop(0, n_started, drain, 0)


def _dispatch(u2, pos, tabs, n_row_tiles):
    n_batch, t, d = u2.shape
    tiles = t // ROW_TILE
    pair = pl.BlockSpec((TOP_K, ROW_TILE), lambda b, j, *_: (0, b * tiles + j))
    return pl.pallas_call(
        _dispatch_kernel,
        out_shape=jax.ShapeDtypeStruct((_buffer_chunks(n_row_tiles), CHUNK, d), BF16),
        grid_spec=pltpu.PrefetchScalarGridSpec(
            num_scalar_prefetch=4,
            grid=(n_batch, tiles),
            in_specs=[pl.BlockSpec((1, ROW_TILE, d), lambda b, j, *_: (b, j, 0)), pair],
            out_specs=pl.BlockSpec(memory_space=pl.ANY),
            scratch_shapes=[pltpu.VMEM((2, MAX_CHUNKS, CHUNK, d), BF16), pltpu.VMEM((1, CHUNK, d), BF16),
                            pltpu.SemaphoreType.DMA((2,))],
        ),
        compiler_params=_params(("arbitrary", "arbitrary")),
        name="dispatch",
    )(tabs["dst"], tabs["tail_on"], tabs["tail_lo"], tabs["tail_n"], u2, pos)


def _experts_kernel(te_tab, nv_tab, xs_ref, w1_ref, w3_ref, w2_ref, ys_ref, w1b, w3b, w2b):
    @pl.when(pl.program_id(0) < nv_tab[0])
    def _():
        j = pl.program_id(0)

        @pl.when((j == 0) | (te_tab[j] != te_tab[jnp.maximum(j - 1, 0)]))
        def _():
            w1b[...] = w1_ref[0, 0].astype(BF16)
            w3b[...] = w3_ref[0, 0].astype(BF16)
            w2b[...] = w2_ref[0, 0].astype(BF16)

        x = xs_ref[...]
        h1 = jnp.dot(x, w1b[...], preferred_element_type=F32)
        h3 = jnp.dot(x, w3b[...], preferred_element_type=F32)
        h = (_silu(h1) * h3).astype(BF16)
        ys_ref[...] = jnp.dot(h, w2b[...], preferred_element_type=F32).astype(BF16)


def _experts(xs, w1, w3, w2, layer, tile_expert, n_valid, n_row_tiles):
    rows, d = xs.shape
    de = w1.shape[3]
    row = pl.BlockSpec((EXPERT_ROW_TILE, d), lambda j, te, nv: (jnp.minimum(j, nv[0] - 1), 0))
    return pl.pallas_call(
        _experts_kernel,
        out_shape=jax.ShapeDtypeStruct((rows, d), BF16),
        grid_spec=pltpu.PrefetchScalarGridSpec(
            num_scalar_prefetch=2,
            grid=(n_row_tiles,),
            in_specs=[row,
                      pl.BlockSpec((1, 1, d, de), lambda j, te, nv: (layer, te[j], 0, 0)),
                      pl.BlockSpec((1, 1, d, de), lambda j, te, nv: (layer, te[j], 0, 0)),
                      pl.BlockSpec((1, 1, de, d), lambda j, te, nv: (layer, te[j], 0, 0))],
            out_specs=row,
            scratch_shapes=[pltpu.VMEM((d, de), BF16), pltpu.VMEM((d, de), BF16),
                            pltpu.VMEM((de, d), BF16)],
        ),
        compiler_params=_params(("arbitrary",)),
        name="experts",
    )(tile_expert, n_valid, xs, w1, w3, w2)


def _combine_kernel(src_tab, tail_on,
                    ys_hbm, pos_ref, wts_ref, x1_ref, u2_ref, mod_ref, sw1_ref, sw3_ref, sw2_ref,
                    g_ref, b_ref, o_ref, loc, acc, sems, *, n_batch, n_ctx_tiles, d, alpha):
    i = _flat_tile()
    n_tiles = pl.num_programs(0) * pl.num_programs(1)
    slot = i % COMBINE_SLOTS
    ahead = COMBINE_SLOTS - 1
    cpb = ROW_TILE // CHUNK
    tail_blocks = range(STATIC_BLOCKS, LOCAL_ROWS // ROW_TILE)

    def fetch(tile, sl, chunks):
        for c in chunks:
            _chunk_copy(ys_hbm, src_tab[tile * MAX_CHUNKS + c], loc.at[sl], c, sems.at[sl]).start()

    def fetch_tail(tile, sl):
        @pl.when(tail_on[tile] != 0)
        def _():
            fetch(tile, sl, range(STATIC_CHUNKS, MAX_CHUNKS))

    def partial_sum(blocks, spread_fetch):
        pos = pos_ref[...]
        wts = wts_ref[...].astype(BF16)
        total = jnp.zeros((ROW_TILE, d), F32)
        for rb in blocks:
            rows = loc[slot, rb * cpb:(rb + 1) * cpb].reshape(ROW_TILE, d)
            sel_t = _selection_block(pos, wts, rb).T
            total = total + jnp.dot(sel_t, rows, preferred_element_type=F32)
            if spread_fetch:
                fetch(nxt, nxt_slot, range(rb * cpb, (rb + 1) * cpb))
        return total

    @pl.when(i == 0)
    def _():
        for s in range(ahead):
            fetch(s % n_tiles, s, range(STATIC_CHUNKS))
            fetch_tail(s % n_tiles, s)

    _wait_tile(tail_on, i, loc.at[slot], ys_hbm, sems.at[slot], to_global=False)

    nxt = (i + ahead) % n_tiles
    nxt_slot = (i + ahead) % COMBINE_SLOTS
    acc[...] = partial_sum(range(STATIC_BLOCKS), spread_fetch=True)
    fetch_tail(nxt, nxt_slot)

    @pl.when(tail_on[i] != 0)
    def _():
        acc[...] += partial_sum(tail_blocks, spread_fetch=False)

    u = u2_ref[0]
    hs = (_silu(jnp.dot(u, sw1_ref[...], preferred_element_type=F32))
          * jnp.dot(u, sw3_ref[...], preferred_element_type=F32)).astype(BF16)
    shared = jnp.dot(hs, sw2_ref[...], preferred_element_type=F32)

    m = _mod_row(mod_ref, n_batch, n_ctx_tiles)
    g2 = m[:, 5 * d:6 * d]
    o_ref[0] = _layer_norm(alpha * x1_ref[0] + g2 * (acc[...] + shared), g_ref[...], b_ref[...])

    @pl.when(i == n_tiles - 1)
    def _():
        for s in range(1, COMBINE_SLOTS):
            sl = (i + s) % COMBINE_SLOTS
            _wait_tile(tail_on, (i + s) % n_tiles, loc.at[sl], ys_hbm, sems.at[sl], to_global=False)


def _combine(ys, pos, wts, x1, u2, mod_l, sw1, sw3, sw2, ln_g, ln_b, tabs, n_ctx, alpha, latent_only):
    n_batch, t, d = x1.shape
    tiles = t // ROW_TILE
    n_ctx_tiles = n_ctx // ROW_TILE
    row = pl.BlockSpec((1, ROW_TILE, d), lambda b, j, *_: (b, j, 0))
    pair = pl.BlockSpec((TOP_K, ROW_TILE), lambda b, j, *_: (0, b * tiles + j))
    out_rows, out_row = t, row
    if latent_only:
        out_rows = t - n_ctx
        out_row = pl.BlockSpec((1, ROW_TILE, d), lambda b, j, *_: (b, jnp.maximum(j - n_ctx_tiles, 0), 0))
    return pl.pallas_call(
        functools.partial(_combine_kernel, n_batch=n_batch, n_ctx_tiles=n_ctx_tiles, d=d,
                          alpha=alpha),
        out_shape=jax.ShapeDtypeStruct((n_batch, out_rows, d), F32),
        grid_spec=pltpu.PrefetchScalarGridSpec(
            num_scalar_prefetch=2,
            grid=(n_batch, tiles),
            in_specs=[pl.BlockSpec(memory_space=pl.ANY), pair, pair,
                      row, row] + [_const_spec(a.shape) for a in (mod_l, sw1, sw3, sw2, ln_g, ln_b)],
            out_specs=out_row,
            scratch_shapes=[pltpu.VMEM((COMBINE_SLOTS, MAX_CHUNKS, CHUNK, d), BF16),
                            pltpu.VMEM((ROW_TILE, d), F32),
                            pltpu.SemaphoreType.DMA((COMBINE_SLOTS,))],
        ),
        compiler_params=_params(("arbitrary", "arbitrary")),
        name="combine",
    )(tabs["src"], tabs["tail_on"], ys, pos, wts, x1, u2, mod_l, sw1, sw3, sw2, ln_g, ln_b)


def _dispatch_tables(nch, lo, n_row_tiles):
    cpt = EXPERT_ROW_TILE // CHUNK
    tot_e = jnp.sum(nch, axis=0)
    region = (tot_e + cpt - 1) // cpt * cpt
    ends = jnp.cumsum(region)
    base = ends - region
    g = base[None, :] + jnp.cumsum(nch, axis=0) - nch
    c = jnp.arange(MAX_CHUNKS, dtype=nch.dtype)[None, :, None]
    covers = (lo[:, None, :] <= c) & (c < (lo + nch)[:, None, :])
    used = jnp.any(covers, axis=-1)
    dst = c[:, :, 0] + jnp.sum(jnp.where(covers, (g - lo)[:, None, :], 0), axis=-1)
    slot = (jnp.arange(nch.shape[0], dtype=nch.dtype) % 2)[:, None]
    spare = n_row_tiles * cpt + slot * MAX_CHUNKS + c[:, :, 0]
    row_tile = jnp.arange(n_row_tiles, dtype=nch.dtype)[:, None]
    tile_expert = jnp.sum((ends // cpt)[None, :] <= row_tile, axis=-1)
    i32 = lambda a: a.astype(jnp.int32)
    tabs = dict(dst=i32(jnp.where(used, dst, spare).reshape(-1)),
                src=i32(jnp.where(used, dst, 0).reshape(-1)),
                tail_on=i32(jnp.sum(nch, axis=1) > STATIC_CHUNKS),
                tail_lo=i32(base + tot_e), tail_n=i32(region - tot_e))
    return tabs, i32(jnp.minimum(tile_expert, N_EXPERTS - 1)), i32(ends[-1:] // cpt)


def _buffer_chunks(n_row_tiles):
    return n_row_tiles * (EXPERT_ROW_TILE // CHUNK) + 2 * MAX_CHUNKS


def _rope_tables(n_ctx, n_lat):
    pos = jnp.arange(n_lat)
    n_freq = HEAD_DIM // 4
    inv = ROPE_THETA ** (-jnp.arange(n_freq, dtype=F32) / n_freq)
    ang_r = (pos // GRID_W).astype(F32)[:, None] * inv
    ang_c = (pos % GRID_W).astype(F32)[:, None] * inv
    cos_l = jnp.concatenate([jnp.cos(ang_r)] * 2 + [jnp.cos(ang_c)] * 2, axis=1)
    sin_l = jnp.concatenate([-jnp.sin(ang_r), jnp.sin(ang_r), -jnp.sin(ang_c), jnp.sin(ang_c)], axis=1)
    cos_t = jnp.concatenate([jnp.ones((n_ctx, HEAD_DIM), F32), cos_l], axis=0)
    sin_t = jnp.concatenate([jnp.zeros((n_ctx, HEAD_DIM), F32), sin_l], axis=0)
    return cos_t, sin_t


def _pack_gate_weights(wa, wi):
    n_dir, n_blocks, w, _ = wa.shape
    per_tile = MXU_TILE // w
    n_tiles = n_blocks // per_tile

    def dense(wx):
        wx = wx.reshape(n_dir, n_tiles, per_tile, w, w)
        eye = jnp.eye(per_tile, dtype=wx.dtype)
        full = jnp.einsum('dtpij,pq->dtpiqj', wx, eye)
        return full.reshape(n_dir, n_tiles, MXU_TILE, MXU_TILE)

    return jnp.concatenate([dense(wa), dense(wi)], axis=-1).astype(BF16)


def kernel(x, c, ctx, c_ctx, w_mod, b_mod, w_in, sink, conv_w, conv_b, rg_wa, rg_ba, rg_wi, rg_bi,
           rg_lambda, w_o_attn, w_o_rnn, w_out, ln1_g, ln1_b, router_w, router_bias, exp_w1, exp_w3,
           exp_w2, sh_w1, sh_w3, sh_w2, ln2_g, ln2_b):
    n_batch, n_lat, d = x.shape
    n_ctx = ctx.shape[1]
    n_layers = w_mod.shape[0]
    t = n_ctx + n_lat
    assert n_ctx % ROW_TILE == 0 and n_lat % ROW_TILE == 0 and n_batch + 1 <= MOD_ROWS
    assert exp_w1.shape[1:] == (N_EXPERTS, d, D_EXPERT) and sh_w1.shape[1:] == (d, D_EXPERT)
    alpha = (2 * n_layers) ** 0.25
    n_tiles = n_batch * t // ROW_TILE
    chunks_per_row_tile = EXPERT_ROW_TILE // CHUNK
    max_chunks = (TOP_K * n_batch * t // CHUNK + n_tiles * N_EXPERTS
                  + N_EXPERTS * (chunks_per_row_tile - 1))
    n_row_tiles = pl.cdiv(max_chunks, chunks_per_row_tile)

    xa = jnp.concatenate([ctx, x], axis=1)
    cc = jnp.zeros((MOD_ROWS, d), F32).at[:n_batch].set(c).at[n_batch].set(c_ctx)
    mod = _modulation(cc, w_mod, b_mod)
    cos_t, sin_t = _rope_tables(n_ctx, n_lat)
    mask_bias = _attn_mask_bias(n_ctx)

    for l in range(n_layers):
        mod_l = mod[l]
        q, k, v, xr, gy, sa, sr = _in_projection(xa, mod_l, w_in[l].astype(BF16), cos_t, sin_t, n_ctx)
        attn = _attention(sink[l], q, k, v, mask_bias, n_ctx)
        wg = _pack_gate_weights(rg_wa[l], rg_wi[l])
        hf, hb = _rnn_branch(xr, conv_w[l], conv_b[l][None, :], wg, rg_ba[l], rg_bi[l], rg_lambda[l],
                             n_ctx)
        rw_t = router_w[l].T
        rw_hi = rw_t.astype(BF16)
        rw_lo = (rw_t - rw_hi.astype(F32)).astype(BF16)
        x1, u2, logits_t = _out_projection(
            xa, attn, hf, hb, gy, sa, sr, mod_l, w_o_attn[l].astype(BF16), w_o_rnn[l].astype(BF16),
            w_out[l].astype(BF16), ln1_g[l][None, :], ln1_b[l][None, :], rw_hi, rw_lo, n_ctx, alpha)
        pos, wts, nch, lo = _route(logits_t, router_bias[l][:, None])
        tabs, tile_expert, n_valid = _dispatch_tables(nch[:, :, 0], lo[:, :, 0], n_row_tiles)
        xs = _dispatch(u2, pos, tabs, n_row_tiles)
        ys = _experts(xs.reshape(-1, d), exp_w1, exp_w3, exp_w2, l, tile_expert, n_valid, n_row_tiles)
        xa = _combine(ys.reshape(xs.shape), pos, wts, x1, u2, mod_l, sh_w1[l].astype(BF16), sh_w3[l].astype(BF16),
                      sh_w2[l].astype(BF16), ln2_g[l][None, :], ln2_b[l][None, :], tabs, n_ctx, alpha,
                      latent_only=(l == n_layers - 1))
    return xa
```

```python
import functools
import math

import jax
import jax.numpy as jnp
from jax import lax
from jax.experimental import pallas as pl
from jax.experimental.pallas import tpu as pltpu

N_HEADS = 8
N_KV_HEADS = 2
HEAD_DIM = 128
KV_GROUP = N_HEADS // N_KV_HEADS
ATT_BLOCK = 128
ATT_QBLOCKS = 2
GRID_W = 64
ROPE_THETA = 10000.0
RNN_BLOCKS = 16
LRU_C = 8.0
N_EXPERTS = 64
TOP_K = 8
N_GROUPS = 8
TOPK_GROUPS = 4
D_EXPERT = 256
ROUTE_SCALE = 2.5
LN_EPS = 1e-6

ROW_TILE = 256
MXU_TILE = 256
CHUNK = 16
LOCAL_ROWS = TOP_K * ROW_TILE + N_EXPERTS * CHUNK
MAX_CHUNKS = LOCAL_ROWS // CHUNK
STATIC_BLOCKS = 10
STATIC_CHUNKS = STATIC_BLOCKS * ROW_TILE // CHUNK
EXPERT_ROW_TILE = 1024
COMBINE_SLOTS = 3
MOD_ROWS = 24
VMEM_LIMIT = 56 * 1024 * 1024

F32 = jnp.float32
BF16 = jnp.bfloat16
NEG_BIG = -1e30


def _const_spec(shape):
    zeros = (0,) * len(shape)
    return pl.BlockSpec(shape, lambda *_: zeros, pipeline_mode=pl.Buffered(1))


def _params(sem):
    return pltpu.CompilerParams(dimension_semantics=sem, vmem_limit_bytes=VMEM_LIMIT)


def _silu(v):
    return v * jax.nn.sigmoid(v)


def _gelu_tanh(v):
    return v * (0.5 * (1.0 + jnp.tanh(math.sqrt(2.0 / math.pi) * (v + 0.044715 * (v * v * v)))))


def _layer_norm(v, g, b):
    mu = jnp.mean(v, axis=-1, keepdims=True)
    d = v - mu
    var = jnp.mean(d * d, axis=-1, keepdims=True)
    return d * lax.rsqrt(var + LN_EPS) * g + b


def _mod_kernel(cc_ref, w_ref, b_ref, o_ref):
    s = _silu(cc_ref[...]).astype(BF16)
    o_ref[0] = jnp.dot(s, w_ref[0].astype(BF16), preferred_element_type=F32) + b_ref[0]


def _modulation(cc, w_mod, b_mod):
    n_layers, d, d6 = w_mod.shape
    col = 1536
    return pl.pallas_call(
        _mod_kernel,
        out_shape=jax.ShapeDtypeStruct((n_layers, MOD_ROWS, d6), F32),
        grid=(n_layers, d6 // col),
        in_specs=[
            pl.BlockSpec((MOD_ROWS, d), lambda l, j: (0, 0)),
            pl.BlockSpec((1, d, col), lambda l, j: (l, 0, j)),
            pl.BlockSpec((1, 1, col), lambda l, j: (l, 0, j)),
        ],
        out_specs=pl.BlockSpec((1, MOD_ROWS, col), lambda l, j: (l, 0, j)),
        compiler_params=_params(("parallel", "parallel")),
        name="modulation",
    )(cc, w_mod, b_mod.reshape(n_layers, 1, d6))


def _tiles_per_step(tiles):
    return next(s for s in (3, 2, 1) if tiles % s == 0)


def _mod_row(mod_ref, n_batch, n_ctx_tiles):
    b = pl.program_id(0)
    j = pl.program_id(1)
    row = jnp.where(j < n_ctx_tiles, n_batch, b)
    return mod_ref[pl.ds(row, 1), :]


def _rope(v, cos_w, sin_w):
    width = v.shape[1]
    lane = lax.broadcasted_iota(jnp.int32, v.shape, 1)
    partner = jnp.where((lane & 32) == 0, pltpu.roll(v, width - 32, 1), pltpu.roll(v, 32, 1))
    return v * cos_w + partner * sin_w


def _inproj_kernel(x_ref, mod_ref, w_ref, cos_ref, sin_ref,
                   q_ref, k_ref, v_ref, xr_ref, gy_ref, sa_ref, sr_ref,
                   *, n_batch, n_ctx_tiles, d, sub):
    b = pl.program_id(0)
    j = pl.program_id(1)
    parts = []
    for s in range(sub):
        mrow = jnp.where(j * sub + s < n_ctx_tiles, n_batch, b)
        m = mod_ref[pl.ds(mrow, 1), :]
        xs = x_ref[0, s * ROW_TILE:(s + 1) * ROW_TILE, :]
        parts.append((xs * (1.0 + m[:, d:2 * d]) + m[:, 0:d]).astype(BF16))
    u = jnp.concatenate(parts, axis=0)

    def proj(lo, hi):
        return jnp.dot(u, w_ref[:, lo:hi], preferred_element_type=F32)

    qw = N_HEADS * HEAD_DIM
    kw = N_KV_HEADS * HEAD_DIM
    cos1 = cos_ref[...]
    sin1 = sin_ref[...]
    o = 0
    q = proj(o, o + qw) * (HEAD_DIM ** -0.5)
    q_ref[0] = _rope(q, jnp.concatenate([cos1] * N_HEADS, axis=1),
                     jnp.concatenate([sin1] * N_HEADS, axis=1)).astype(BF16)
    o += qw
    k = proj(o, o + kw)
    k_ref[0] = _rope(k, jnp.concatenate([cos1] * N_KV_HEADS, axis=1),
                     jnp.concatenate([sin1] * N_KV_HEADS, axis=1)).astype(BF16)
    o += kw
    v_ref[0] = proj(o, o + kw).astype(BF16)
    o += kw
    xr_ref[0] = proj(o, o + d)
    o += d
    gy_ref[0] = _gelu_tanh(proj(o, o + d)).astype(BF16)
    o += d
    sa_ref[0] = jax.nn.sigmoid(proj(o, o + d)).astype(BF16)
    o += d
    sr_ref[0] = jax.nn.sigmoid(proj(o, o + d)).astype(BF16)


def _in_projection(xa, mod_l, w_in, cos_t, sin_t, n_ctx):
    n_batch, t, d = xa.shape
    qw = N_HEADS * HEAD_DIM
    kw = N_KV_HEADS * HEAD_DIM
    tiles = t // ROW_TILE
    sub = _tiles_per_step(tiles)
    row = lambda w: pl.BlockSpec((1, sub * ROW_TILE, w), lambda b, j: (b, j, 0))
    tab = pl.BlockSpec((sub * ROW_TILE, HEAD_DIM), lambda b, j: (j, 0))
    shp = lambda w, dt: jax.ShapeDtypeStruct((n_batch, t, w), dt)
    return pl.pallas_call(
        functools.partial(_inproj_kernel, n_batch=n_batch, n_ctx_tiles=n_ctx // ROW_TILE, d=d,
                          sub=sub),
        out_shape=(shp(qw, BF16), shp(kw, BF16), shp(kw, BF16), shp(d, F32),
                   shp(d, BF16), shp(d, BF16), shp(d, BF16)),
        grid=(n_batch, tiles // sub),
        in_specs=[row(d), _const_spec(mod_l.shape), _const_spec(w_in.shape), tab, tab],
        out_specs=(row(qw), row(kw), row(kw), row(d), row(d), row(d), row(d)),
        compiler_params=_params(("parallel", "parallel")),
        name="in_projection",
    )(xa, mod_l, w_in, cos_t, sin_t)


def _attn_mask_bias(n_ctx):
    n_win = 3 * ATT_BLOCK
    rr = jnp.arange(ATT_BLOCK)[:, None]
    jj = jnp.arange(n_win + n_ctx)[None, :]
    in_band = (jj >= rr) & (jj <= rr + 2 * ATT_BLOCK)
    is_ctx = jj >= n_win
    kinds = [is_ctx | jnp.zeros_like(in_band)]
    for has_left, has_right in ((False, False), (True, False), (False, True), (True, True)):
        ok = in_band & (has_left | (jj >= ATT_BLOCK)) & (has_right | (jj < 2 * ATT_BLOCK))
        kinds.append(is_ctx | ok)
    return jnp.where(jnp.stack(kinds), 0.0, NEG_BIG).astype(F32)


def _attn_kernel(sink_ref, q_ref, k0_ref, k1_ref, k2_ref, k3_ref, kx_ref,
                 v0_ref, v1_ref, v2_ref, v3_ref, vx_ref, bias_ref, o_ref, *, n_ctx_blocks, n_lat_blocks):
    rows = KV_GROUP * ATT_BLOCK
    rblk = lax.broadcasted_iota(jnp.int32, (rows, 1), 0) // ATT_BLOCK
    k_blocks = (k0_ref, k1_ref, k2_ref, k3_ref)
    v_blocks = (v0_ref, v1_ref, v2_ref, v3_ref)
    for sb in range(ATT_QBLOCKS):
        qrows = slice(sb * ATT_BLOCK, (sb + 1) * ATT_BLOCK)
        i = pl.program_id(1) * ATT_QBLOCKS + sb - n_ctx_blocks
        kind = jnp.where(i < 0, 0,
                         1 + (i > 0).astype(jnp.int32) + 2 * (i < n_lat_blocks - 1).astype(jnp.int32))
        bias = jnp.concatenate([bias_ref[kind]] * KV_GROUP, axis=0)
        for h in range(N_KV_HEADS):
            hs = slice(h * HEAD_DIM, (h + 1) * HEAD_DIM)
            kk = jnp.concatenate([r[0, :, hs] for r in k_blocks[sb:sb + 3]] + [kx_ref[0, :, hs]], axis=0)
            vv = jnp.concatenate([r[0, :, hs] for r in v_blocks[sb:sb + 3]] + [vx_ref[0, :, hs]], axis=0)
            heads = [h * KV_GROUP + g for g in range(KV_GROUP)]
            qs = jnp.concatenate([q_ref[0, qrows, hd * HEAD_DIM:(hd + 1) * HEAD_DIM] for hd in heads],
                                 axis=0)
            s = lax.dot_general(qs, kk, (((1,), (1,)), ((), ())), preferred_element_type=F32) + bias
            sink = jnp.zeros((rows, 1), F32)
            for g, hd in enumerate(heads):
                sink = jnp.where(rblk == g, sink_ref[hd], sink)
            mx = jnp.maximum(jnp.max(s, axis=1, keepdims=True), sink)
            p = jnp.exp(s - mx)
            den = jnp.sum(p, axis=1, keepdims=True) + jnp.exp(sink - mx)
            o = jnp.dot(p.astype(BF16), vv, preferred_element_type=F32) / den
            for g, hd in enumerate(heads):
                o_ref[0, qrows, hd * HEAD_DIM:(hd + 1) * HEAD_DIM] = (
                    o[g * ATT_BLOCK:(g + 1) * ATT_BLOCK].astype(BF16))


def _attention(sink_l, q, k, v, mask_bias, n_ctx):
    n_batch, t, qw = q.shape
    kw = k.shape[2]
    ncb = n_ctx // ATT_BLOCK
    nlb = (t - n_ctx) // ATT_BLOCK

    assert ncb % ATT_QBLOCKS == 0 and nlb % ATT_QBLOCKS == 0

    def lat_blk(off):
        def index(b, j, sink):
            return (b, ncb + jnp.clip(j * ATT_QBLOCKS - ncb + off, 0, nlb - 1), 0)
        return pl.BlockSpec((1, ATT_BLOCK, kw), index)

    ctx_blk = pl.BlockSpec((1, n_ctx, kw), lambda b, j, sink: (b, 0, 0))
    q_blk = pl.BlockSpec((1, ATT_QBLOCKS * ATT_BLOCK, qw), lambda b, j, sink: (b, j, 0))
    kv_specs = [lat_blk(off) for off in range(-1, ATT_QBLOCKS + 1)] + [ctx_blk]
    return pl.pallas_call(
        functools.partial(_attn_kernel, n_ctx_blocks=ncb, n_lat_blocks=nlb),
        out_shape=jax.ShapeDtypeStruct((n_batch, t, qw), BF16),
        grid_spec=pltpu.PrefetchScalarGridSpec(
            num_scalar_prefetch=1,
            grid=(n_batch, t // (ATT_QBLOCKS * ATT_BLOCK)),
            in_specs=[q_blk] + kv_specs + kv_specs + [_const_spec(mask_bias.shape)],
            out_specs=q_blk,
        ),
        compiler_params=_params(("parallel", "parallel")),
        name="attention",
    )(sink_l, q, *([k] * len(kv_specs)), *([v] * len(kv_specs)), mask_bias)


def _rnn_kernel(x_ref, cw_ref, cb_ref, wg_ref, ba_ref, bi_ref, lam_ref, hf_ref, hb_ref,
                a_scr, b_scr, h_scr, *, n_ctx, n_rows):
    j = pl.program_id(1)
    tb = ROW_TILE
    n_blk = n_rows // tb
    n_cblk = n_ctx // tb
    blk_f = j
    blk_b = jnp.where(j < n_cblk, n_cblk - 1 - j, n_blk - 1 - (j - n_cblk))
    d = x_ref.shape[2]
    ridx = lax.broadcasted_iota(jnp.int32, (tb, d), 0)

    def conv_block(blk):
        t0 = pl.multiple_of(blk * tb, tb)
        cur = x_ref[0, pl.ds(t0, tb), :]
        prev8 = x_ref[0, pl.ds(pl.multiple_of(jnp.maximum(t0 - 8, 0), 8), 8), :]
        next8 = x_ref[0, pl.ds(pl.multiple_of(jnp.minimum(t0 + tb, n_rows - 8), 8), 8), :]
        prev_ok = (t0 != 0) & (t0 != n_ctx)
        next_ok = (t0 + tb != n_ctx) & (t0 + tb != n_rows)
        prev8 = jnp.where(prev_ok, prev8, 0.0)
        next8 = jnp.where(next_ok, next8, 0.0)
        p6 = jnp.broadcast_to(prev8[6:7, :], (tb, d))
        p7 = jnp.broadcast_to(prev8[7:8, :], (tb, d))
        n0 = jnp.broadcast_to(next8[0:1, :], (tb, d))
        xm1 = jnp.where(ridx == 0, p7, pltpu.roll(cur, 1, 0))
        xm2 = jnp.where(ridx == 0, p6, jnp.where(ridx == 1, p7, pltpu.roll(cur, 2, 0)))
        xp1 = jnp.where(ridx == tb - 1, n0, pltpu.roll(cur, tb - 1, 0))
        return (cb_ref[...] + xm2 * cw_ref[0:1, :] + xm1 * cw_ref[1:2, :]
                + cur * cw_ref[2:3, :] + xp1 * cw_ref[3:4, :])

    def gate_block(xc, dr):
        xb = xc.astype(BF16)
        r_parts, i_parts = [], []
        for jt in range(d // MXU_TILE):
            g = jnp.dot(xb[:, jt * MXU_TILE:(jt + 1) * MXU_TILE], wg_ref[dr, jt],
                        preferred_element_type=F32)
            r_parts.append(g[:, :MXU_TILE])
            i_parts.append(g[:, MXU_TILE:])
        r = jax.nn.sigmoid(jnp.concatenate(r_parts, axis=1) + ba_ref[dr:dr + 1, :])
        ig = jax.nn.sigmoid(jnp.concatenate(i_parts, axis=1) + bi_ref[dr:dr + 1, :])
        nl = -lam_ref[dr:dr + 1, :]
        softplus = jnp.maximum(nl, 0.0) + jnp.log1p(jnp.exp(-jnp.abs(nl)))
        log_a = (-LRU_C) * r * softplus
        a = jnp.exp(log_a)
        a_scr[dr] = a
        gap = 1.0 - a * a
        root = jnp.where(gap > 0.0, gap * lax.rsqrt(gap), 0.0)
        b_scr[dr] = root * (ig * xc)

    gate_block(conv_block(blk_f), 0)
    gate_block(conv_block(blk_b), 1)

    @pl.when(j == 0)
    def _():
        h_scr[...] = jnp.zeros_like(h_scr)

    def step(s, carry):
        hf, hb = carry
        tf = s
        tr = tb - 1 - s
        hf = a_scr[0, pl.ds(tf, 1), :] * hf + b_scr[0, pl.ds(tf, 1), :]
        hb = a_scr[1, pl.ds(tr, 1), :] * hb + b_scr[1, pl.ds(tr, 1), :]
        hf_ref[0, pl.ds(tf, 1), :] = hf
        hb_ref[0, pl.ds(tr, 1), :] = hb
        return hf, hb

    hf, hb = lax.fori_loop(0, tb, step, (h_scr[0:1, :], h_scr[1:2, :]), unroll=8)
    h_scr[0:1, :] = hf
    h_scr[1:2, :] = hb


def _rnn_branch(xr, conv_w, conv_b, wg, ba, bi, lam, n_ctx):
    n_batch, t, d = xr.shape
    n_blk = t // ROW_TILE
    n_cblk = n_ctx // ROW_TILE

    def bwd_index(b, j):
        return (b, jnp.where(j < n_cblk, n_cblk - 1 - j, n_blk - 1 - (j - n_cblk)), 0)

    out = jax.ShapeDtypeStruct((n_batch, t, d), F32)
    return pl.pallas_call(
        functools.partial(_rnn_kernel, n_ctx=n_ctx, n_rows=t),
        out_shape=(out, out),
        grid=(n_batch, n_blk),
        in_specs=[
            pl.BlockSpec((1, t, d), lambda b, j: (b, 0, 0)),
            _const_spec(conv_w.shape), _const_spec(conv_b.shape), _const_spec(wg.shape),
            _const_spec(ba.shape), _const_spec(bi.shape), _const_spec(lam.shape),
        ],
        out_specs=(pl.BlockSpec((1, ROW_TILE, d), lambda b, j: (b, j, 0)),
                   pl.BlockSpec((1, ROW_TILE, d), bwd_index)),
        scratch_shapes=[pltpu.VMEM((2, ROW_TILE, d), F32), pltpu.VMEM((2, ROW_TILE, d), F32),
                        pltpu.VMEM((8, d), F32)],
        compiler_params=_params(("parallel", "arbitrary")),
        name="rnn_branch",
    )(xr, conv_w, conv_b, wg, ba, bi, lam)


def _dot_nt(a, b):
    return lax.dot_general(a, b, (((1,), (1,)), ((), ())), preferred_element_type=F32)


def _outproj_kernel(x_ref, attn_ref, hf_ref, hb_ref, gy_ref, sa_ref, sr_ref, mod_ref,
                    woa_ref, wor_ref, wout_ref, g_ref, b_ref, rwh_ref, rwl_ref,
                    x1_ref, u2_ref, lg_ref, *, n_batch, n_ctx_tiles, d, alpha, sub):
    b = pl.program_id(0)
    j = pl.program_id(1)
    for s in range(sub):
        rs = slice(s * ROW_TILE, (s + 1) * ROW_TILE)
        mrow = jnp.where(j * sub + s < n_ctx_tiles, n_batch, b)
        m = mod_ref[pl.ds(mrow, 1), :]
        g1 = m[:, 2 * d:3 * d]
        sh2 = m[:, 3 * d:4 * d]
        sc2 = m[:, 4 * d:5 * d]
        a = jnp.dot(attn_ref[0, rs, :], woa_ref[...], preferred_element_type=F32)
        rnn = ((hf_ref[0, rs, :] + hb_ref[0, rs, :]) * gy_ref[0, rs, :].astype(F32)).astype(BF16)
        r = jnp.dot(rnn, wor_ref[...], preferred_element_type=F32)
        merged = (sa_ref[0, rs, :].astype(F32) * a + sr_ref[0, rs, :].astype(F32) * r).astype(BF16)
        mix = jnp.dot(merged, wout_ref[...], preferred_element_type=F32)
        x1 = _layer_norm(alpha * x_ref[0, rs, :] + g1 * mix, g_ref[...], b_ref[...])
        x1_ref[0, rs, :] = x1
        u2 = x1 * (1.0 + sc2) + sh2
        u_hi = u2.astype(BF16)
        u2_ref[0, rs, :] = u_hi
        u_lo = (u2 - u_hi.astype(F32)).astype(BF16)
        lg_ref[:, rs] = (_dot_nt(rwh_ref[...], u_hi) + _dot_nt(rwh_ref[...], u_lo)
                         + _dot_nt(rwl_ref[...], u_hi))


def _out_projection(xa, attn, hf, hb, gy, sa, sr, mod_l, woa, wor, wout, ln_g, ln_b, rwh, rwl,
                    n_ctx, alpha):
    n_batch, t, d = xa.shape
    tiles = t // ROW_TILE
    sub = _tiles_per_step(tiles)
    steps = tiles // sub
    row = pl.BlockSpec((1, sub * ROW_TILE, d), lambda b, j: (b, j, 0))
    return pl.pallas_call(
        functools.partial(_outproj_kernel, n_batch=n_batch, n_ctx_tiles=n_ctx // ROW_TILE, d=d,
                          alpha=alpha, sub=sub),
        out_shape=(jax.ShapeDtypeStruct((n_batch, t, d), F32),
                   jax.ShapeDtypeStruct((n_batch, t, d), BF16),
                   jax.ShapeDtypeStruct((N_EXPERTS, n_batch * t), F32)),
        grid=(n_batch, steps),
        in_specs=[row] * 7 + [_const_spec(a.shape) for a in
                              (mod_l, woa, wor, wout, ln_g, ln_b, rwh, rwl)],
        out_specs=(row, row,
                   pl.BlockSpec((N_EXPERTS, sub * ROW_TILE), lambda b, j: (0, b * steps + j))),
        compiler_params=_params(("parallel", "parallel")),
        name="out_projection",
    )(xa, attn, hf, hb, gy, sa, sr, mod_l, woa, wor, wout, ln_g, ln_b, rwh, rwl)


def _route_kernel(lg_ref, bias_ref, pos_ref, wts_ref, nch_ref, lo_ref):
    lg = lg_ref[...]
    n_tok = lg.shape[1]
    per_group = N_EXPERTS // N_GROUPS
    scores = jax.nn.sigmoid(lg)
    biased = scores + bias_ref[...]
    b3 = biased.reshape(N_GROUPS, per_group, n_tok)
    sub = lax.broadcasted_iota(jnp.int32, b3.shape, 1)
    m1 = jnp.max(b3, axis=1, keepdims=True)
    i1 = jnp.min(jnp.where(b3 == m1, sub, per_group), axis=1, keepdims=True)
    m2 = jnp.max(jnp.where(sub == i1, -jnp.inf, b3), axis=1, keepdims=True)
    grp = (m1 + m2).reshape(N_GROUPS, n_tok)

    gi = lax.broadcasted_iota(jnp.int32, grp.shape, 0)
    gsel = jnp.zeros(grp.shape, F32)
    for _ in range(TOPK_GROUPS):
        m = jnp.max(grp, axis=0, keepdims=True)
        idx = jnp.min(jnp.where(grp == m, gi, N_GROUPS), axis=0, keepdims=True)
        hit = gi == idx
        gsel = jnp.where(hit, 1.0, gsel)
        grp = jnp.where(hit, -jnp.inf, grp)
    emask = jnp.broadcast_to(gsel.reshape(N_GROUPS, 1, n_tok), b3.shape).reshape(N_EXPERTS, n_tok)
    cand = jnp.where(emask > 0.0, biased, -jnp.inf)

    ei = lax.broadcasted_iota(jnp.int32, cand.shape, 0)
    comb = jnp.zeros(cand.shape, F32)
    picked = jnp.zeros(cand.shape, F32)
    hits = []
    for _ in range(TOP_K):
        m = jnp.max(cand, axis=0, keepdims=True)
        idx = jnp.min(jnp.where(cand == m, ei, N_EXPERTS), axis=0, keepdims=True)
        hit = ei == idx
        hits.append(hit)
        comb = jnp.where(hit, scores, comb)
        picked = jnp.where(hit, 1.0, picked)
        cand = jnp.where(hit, -jnp.inf, cand)
    comb = comb / jnp.sum(comb, axis=0, keepdims=True) * ROUTE_SCALE

    cnt = jnp.sum(picked, axis=1, keepdims=True)
    nch = jnp.broadcast_to(jnp.floor((cnt + (CHUNK - 1)) * (1.0 / CHUNK)), (N_EXPERTS, 128))
    erow = lax.broadcasted_iota(jnp.int32, nch.shape, 0)
    incl = nch
    for s in (1, 2, 4, 8, 16, 32):
        incl = incl + jnp.where(erow >= s, pltpu.roll(incl, s, 0), 0.0)
    lo = incl - nch
    earlier = (lax.broadcasted_iota(jnp.int32, (n_tok, n_tok), 0)
               < lax.broadcasted_iota(jnp.int32, (n_tok, n_tok), 1)).astype(BF16)
    rank = jnp.dot(picked.astype(BF16), earlier, preferred_element_type=F32)
    slot = lo[:, 0:1] * CHUNK + rank
    pos = jnp.concatenate([jnp.sum(jnp.where(h, slot, 0.0), axis=0, keepdims=True) for h in hits], axis=0)
    wts = jnp.concatenate([jnp.sum(jnp.where(h, comb, 0.0), axis=0, keepdims=True) for h in hits], axis=0)
    pos_ref[...] = pos.astype(jnp.int32)
    wts_ref[...] = wts
    nch_ref[0] = nch.astype(jnp.int32)
    lo_ref[0] = lo.astype(jnp.int32)


def _route(logits_t, bias_col):
    n_tok = logits_t.shape[1]
    n_tiles = n_tok // ROW_TILE
    tab = jax.ShapeDtypeStruct((n_tiles, N_EXPERTS, 128), jnp.int32)
    tab_spec = pl.BlockSpec((1, N_EXPERTS, 128), lambda i: (i, 0, 0))
    pair_spec = pl.BlockSpec((TOP_K, ROW_TILE), lambda i: (0, i))
    return pl.pallas_call(
        _route_kernel,
        out_shape=(jax.ShapeDtypeStruct((TOP_K, n_tok), jnp.int32),
                   jax.ShapeDtypeStruct((TOP_K, n_tok), F32), tab, tab),
        grid=(n_tiles,),
        in_specs=[pl.BlockSpec((N_EXPERTS, ROW_TILE), lambda i: (0, i)), _const_spec(bias_col.shape)],
        out_specs=(pair_spec, pair_spec, tab_spec, tab_spec),
        compiler_params=_params(("parallel",)),
        name="route",
    )(logits_t, bias_col)


def _chunk_copy(src_ref, src_chunk, dst_ref, dst_chunk, sem):
    return pltpu.make_async_copy(src_ref.at[src_chunk], dst_ref.at[dst_chunk], sem)


def _flat_tile():
    return pl.program_id(0) * pl.num_programs(1) + pl.program_id(1)


def _wait_slot(local_slot_ref, global_ref, sem, to_global, tail):
    first, count = (STATIC_CHUNKS, MAX_CHUNKS - STATIC_CHUNKS) if tail else (0, STATIC_CHUNKS)
    local = local_slot_ref.at[pl.ds(first, count)]
    remote = global_ref.at[pl.ds(0, count)]
    if to_global:
        pltpu.make_async_copy(local, remote, sem).wait()
    else:
        pltpu.make_async_copy(remote, local, sem).wait()


def _wait_tile(tail_on, tile, local_slot_ref, global_ref, sem, to_global):
    _wait_slot(local_slot_ref, global_ref, sem, to_global, tail=False)

    @pl.when(tail_on[tile] != 0)
    def _():
        _wait_slot(local_slot_ref, global_ref, sem, to_global, tail=True)


def _selection_block(pos, vals, rb):
    rid = lax.broadcasted_iota(jnp.int32, (ROW_TILE, ROW_TILE), 0).astype(F32).astype(BF16)
    rel = pos - rb * ROW_TILE
    rel = jnp.where(rel < 0, -1, jnp.where(rel >= ROW_TILE, -1, rel)).astype(F32).astype(BF16)
    out = jnp.zeros((ROW_TILE, ROW_TILE), BF16)
    for k in range(TOP_K):
        hit = rid == jnp.broadcast_to(rel[k:k + 1, :], rid.shape)
        val = jnp.ones_like(out) if vals is None else jnp.broadcast_to(vals[k:k + 1, :], rid.shape)
        out = jnp.where(hit, val, out)
    return out


def _dispatch_kernel(dst_tab, tail_on, tail_lo, tail_n,
                     x_ref, pos_ref, xs_hbm, loc, zbuf, sems):
    i = _flat_tile()
    n_tiles = pl.num_programs(0) * pl.num_programs(1)
    slot = i % 2
    cpb = ROW_TILE // CHUNK

    def do_blocks(blocks):
        pos = pos_ref[...]
        x = x_ref[0]
        for rb in blocks:
            p = _selection_block(pos, None, rb)
            loc[slot, rb * cpb:(rb + 1) * cpb] = jnp.dot(
                p, x, preferred_element_type=F32).astype(BF16).reshape(cpb, CHUNK, x.shape[1])
            for c in range(rb * cpb, (rb + 1) * cpb):
                _chunk_copy(loc.at[slot], c, xs_hbm, dst_tab[i * MAX_CHUNKS + c], sems.at[slot]).start()

    do_blocks(range(STATIC_BLOCKS))

    @pl.when(tail_on[i] != 0)
    def _():
        do_blocks(range(STATIC_BLOCKS, LOCAL_ROWS // ROW_TILE))

    @pl.when(i > 0)
    def _():
        _wait_tile(tail_on, jnp.maximum(i - 1, 0), loc.at[1 - slot], xs_hbm, sems.at[1 - slot],
                   to_global=True)

    @pl.when(i == n_tiles - 1)
    def _():
        _wait_tile(tail_on, i, loc.at[slot], xs_hbm, sems.at[slot], to_global=True)
        zbuf[...] = jnp.zeros_like(zbuf)

        def per_expert(e, n_started):
            def per_chunk(c, carry):
                _chunk_copy(zbuf, 0, xs_hbm, tail_lo[e] + c, sems.at[slot]).start()
                return carry

            lax.fori_loop(0, tail_n[e], per_chunk, 0)
            return n_started + tail_n[e]

        n_started = lax.fori_loop(0, N_EXPERTS, per_expert, 0)

        def drain(c, carry):
            _chunk_copy(zbuf, 0, xs_hbm, 0, sems.at[slot]).wait()
            return carry

        lax.fori_loop(0, n_started, drain, 0)


def _dispatch(u2, pos, tabs, n_row_tiles):
    n_batch, t, d = u2.shape
    tiles = t // ROW_TILE
    pair = pl.BlockSpec((TOP_K, ROW_TILE), lambda b, j, *_: (0, b * tiles + j))
    return pl.pallas_call(
        _dispatch_kernel,
        out_shape=jax.ShapeDtypeStruct((_buffer_chunks(n_row_tiles), CHUNK, d), BF16),
        grid_spec=pltpu.PrefetchScalarGridSpec(
            num_scalar_prefetch=4,
            grid=(n_batch, tiles),
            in_specs=[pl.BlockSpec((1, ROW_TILE, d), lambda b, j, *_: (b, j, 0)), pair],
            out_specs=pl.BlockSpec(memory_space=pl.ANY),
            scratch_shapes=[pltpu.VMEM((2, MAX_CHUNKS, CHUNK, d), BF16), pltpu.VMEM((1, CHUNK, d), BF16),
                            pltpu.SemaphoreType.DMA((2,))],
        ),
        compiler_params=_params(("arbitrary", "arbitrary")),
        name="dispatch",
    )(tabs["dst"], tabs["tail_on"], tabs["tail_lo"], tabs["tail_n"], u2, pos)


def _experts_kernel(te_tab, nv_tab, xs_ref, w1_ref, w3_ref, w2_ref, ys_ref, w1b, w3b, w2b):
    @pl.when(pl.program_id(0) < nv_tab[0])
    def _():
        j = pl.program_id(0)

        @pl.when((j == 0) | (te_tab[j] != te_tab[jnp.maximum(j - 1, 0)]))
        def _():
            w1b[...] = w1_ref[0, 0].astype(BF16)
            w3b[...] = w3_ref[0, 0].astype(BF16)
            w2b[...] = w2_ref[0, 0].astype(BF16)

        x = xs_ref[...]
        h1 = jnp.dot(x, w1b[...], preferred_element_type=F32)
        h3 = jnp.dot(x, w3b[...], preferred_element_type=F32)
        h = (_silu(h1) * h3).astype(BF16)
        ys_ref[...] = jnp.dot(h, w2b[...], preferred_element_type=F32).astype(BF16)


def _experts(xs, w1, w3, w2, layer, tile_expert, n_valid, n_row_tiles):
    rows, d = xs.shape
    de = w1.shape[3]
    row = pl.BlockSpec((EXPERT_ROW_TILE, d), lambda j, te, nv: (jnp.minimum(j, nv[0] - 1), 0))
    return pl.pallas_call(
        _experts_kernel,
        out_shape=jax.ShapeDtypeStruct((rows, d), BF16),
        grid_spec=pltpu.PrefetchScalarGridSpec(
            num_scalar_prefetch=2,
            grid=(n_row_tiles,),
            in_specs=[row,
                      pl.BlockSpec((1, 1, d, de), lambda j, te, nv: (layer, te[j], 0, 0)),
                      pl.BlockSpec((1, 1, d, de), lambda j, te, nv: (layer, te[j], 0, 0)),
                      pl.BlockSpec((1, 1, de, d), lambda j, te, nv: (layer, te[j], 0, 0))],
            out_specs=row,
            scratch_shapes=[pltpu.VMEM((d, de), BF16), pltpu.VMEM((d, de), BF16),
                            pltpu.VMEM((de, d), BF16)],
        ),
        compiler_params=_params(("arbitrary",)),
        name="experts",
    )(tile_expert, n_valid, xs, w1, w3, w2)


def _combine_kernel(src_tab, tail_on,
                    ys_hbm, pos_ref, wts_ref, x1_ref, u2_ref, mod_ref, sw1_ref, sw3_ref, sw2_ref,
                    g_ref, b_ref, o_ref, loc, acc, sems, *, n_batch, n_ctx_tiles, d, alpha):
    i = _flat_tile()
    n_tiles = pl.num_programs(0) * pl.num_programs(1)
    slot = i % COMBINE_SLOTS
    ahead = COMBINE_SLOTS - 1
    cpb = ROW_TILE // CHUNK
    tail_blocks = range(STATIC_BLOCKS, LOCAL_ROWS // ROW_TILE)

    def fetch(tile, sl, chunks):
        for c in chunks:
            _chunk_copy(ys_hbm, src_tab[tile * MAX_CHUNKS + c], loc.at[sl], c, sems.at[sl]).start()

    def fetch_tail(tile, sl):
        @pl.when(tail_on[tile] != 0)
        def _():
            fetch(tile, sl, range(STATIC_CHUNKS, MAX_CHUNKS))

    def partial_sum(blocks, spread_fetch):
        pos = pos_ref[...]
        wts = wts_ref[...].astype(BF16)
        total = jnp.zeros((ROW_TILE, d), F32)
        for rb in blocks:
            rows = loc[slot, rb * cpb:(rb + 1) * cpb].reshape(ROW_TILE, d)
            sel_t = _selection_block(pos, wts, rb).T
            total = total + jnp.dot(sel_t, rows, preferred_element_type=F32)
            if spread_fetch:
                fetch(nxt, nxt_slot, range(rb * cpb, (rb + 1) * cpb))
        return total

    @pl.when(i == 0)
    def _():
        for s in range(ahead):
            fetch(s % n_tiles, s, range(STATIC_CHUNKS))
            fetch_tail(s % n_tiles, s)

    _wait_tile(tail_on, i, loc.at[slot], ys_hbm, sems.at[slot], to_global=False)

    nxt = (i + ahead) % n_tiles
    nxt_slot = (i + ahead) % COMBINE_SLOTS
    acc[...] = partial_sum(range(STATIC_BLOCKS), spread_fetch=True)
    fetch_tail(nxt, nxt_slot)

    @pl.when(tail_on[i] != 0)
    def _():
        acc[...] += partial_sum(tail_blocks, spread_fetch=False)

    u = u2_ref[0]
    hs = (_silu(jnp.dot(u, sw1_ref[...], preferred_element_type=F32))
          * jnp.dot(u, sw3_ref[...], preferred_element_type=F32)).astype(BF16)
    shared = jnp.dot(hs, sw2_ref[...], preferred_element_type=F32)

    m = _mod_row(mod_ref, n_batch, n_ctx_tiles)
    g2 = m[:, 5 * d:6 * d]
    o_ref[0] = _layer_norm(alpha * x1_ref[0] + g2 * (acc[...] + shared), g_ref[...], b_ref[...])

    @pl.when(i == n_tiles - 1)
    def _():
        for s in range(1, COMBINE_SLOTS):
            sl = (i + s) % COMBINE_SLOTS
            _wait_tile(tail_on, (i + s) % n_tiles, loc.at[sl], ys_hbm, sems.at[sl], to_global=False)


def _combine(ys, pos, wts, x1, u2, mod_l, sw1, sw3, sw2, ln_g, ln_b, tabs, n_ctx, alpha, latent_only):
    n_batch, t, d = x1.shape
    tiles = t // ROW_TILE
    n_ctx_tiles = n_ctx // ROW_TILE
    row = pl.BlockSpec((1, ROW_TILE, d), lambda b, j, *_: (b, j, 0))
    pair = pl.BlockSpec((TOP_K, ROW_TILE), lambda b, j, *_: (0, b * tiles + j))
    out_rows, out_row = t, row
    if latent_only:
        out_rows = t - n_ctx
        out_row = pl.BlockSpec((1, ROW_TILE, d), lambda b, j, *_: (b, jnp.maximum(j - n_ctx_tiles, 0), 0))
    return pl.pallas_call(
        functools.partial(_combine_kernel, n_batch=n_batch, n_ctx_tiles=n_ctx_tiles, d=d,
                          alpha=alpha),
        out_shape=jax.ShapeDtypeStruct((n_batch, out_rows, d), F32),
        grid_spec=pltpu.PrefetchScalarGridSpec(
            num_scalar_prefetch=2,
            grid=(n_batch, tiles),
            in_specs=[pl.BlockSpec(memory_space=pl.ANY), pair, pair,
                      row, row] + [_const_spec(a.shape) for a in (mod_l, sw1, sw3, sw2, ln_g, ln_b)],
            out_specs=out_row,
            scratch_shapes=[pltpu.VMEM((COMBINE_SLOTS, MAX_CHUNKS, CHUNK, d), BF16),
                            pltpu.VMEM((ROW_TILE, d), F32),
                            pltpu.SemaphoreType.DMA((COMBINE_SLOTS,))],
        ),
        compiler_params=_params(("arbitrary", "arbitrary")),
        name="combine",
    )(tabs["src"], tabs["tail_on"], ys, pos, wts, x1, u2, mod_l, sw1, sw3, sw2, ln_g, ln_b)


def _dispatch_tables(nch, lo, n_row_tiles):
    cpt = EXPERT_ROW_TILE // CHUNK
    tot_e = jnp.sum(nch, axis=0)
    region = (tot_e + cpt - 1) // cpt * cpt
    ends = jnp.cumsum(region)
    base = ends - region
    g = base[None, :] + jnp.cumsum(nch, axis=0) - nch
    c = jnp.arange(MAX_CHUNKS, dtype=nch.dtype)[None, :, None]
    covers = (lo[:, None, :] <= c) & (c < (lo + nch)[:, None, :])
    used = jnp.any(covers, axis=-1)
    dst = c[:, :, 0] + jnp.sum(jnp.where(covers, (g - lo)[:, None, :], 0), axis=-1)
    slot = (jnp.arange(nch.shape[0], dtype=nch.dtype) % 2)[:, None]
    spare = n_row_tiles * cpt + slot * MAX_CHUNKS + c[:, :, 0]
    row_tile = jnp.arange(n_row_tiles, dtype=nch.dtype)[:, None]
    tile_expert = jnp.sum((ends // cpt)[None, :] <= row_tile, axis=-1)
    i32 = lambda a: a.astype(jnp.int32)
    tabs = dict(dst=i32(jnp.where(used, dst, spare).reshape(-1)),
                src=i32(jnp.where(used, dst, 0).reshape(-1)),
                tail_on=i32(jnp.sum(nch, axis=1) > STATIC_CHUNKS),
                tail_lo=i32(base + tot_e), tail_n=i32(region - tot_e))
    return tabs, i32(jnp.minimum(tile_expert, N_EXPERTS - 1)), i32(ends[-1:] // cpt)


def _buffer_chunks(n_row_tiles):
    return n_row_tiles * (EXPERT_ROW_TILE // CHUNK) + 2 * MAX_CHUNKS


def _rope_tables(n_ctx, n_lat):
    pos = jnp.arange(n_lat)
    n_freq = HEAD_DIM // 4
    inv = ROPE_THETA ** (-jnp.arange(n_freq, dtype=F32) / n_freq)
    ang_r = (pos // GRID_W).astype(F32)[:, None] * inv
    ang_c = (pos % GRID_W).astype(F32)[:, None] * inv
    cos_l = jnp.concatenate([jnp.cos(ang_r)] * 2 + [jnp.cos(ang_c)] * 2, axis=1)
    sin_l = jnp.concatenate([-jnp.sin(ang_r), jnp.sin(ang_r), -jnp.sin(ang_c), jnp.sin(ang_c)], axis=1)
    cos_t = jnp.concatenate([jnp.ones((n_ctx, HEAD_DIM), F32), cos_l], axis=0)
    sin_t = jnp.concatenate([jnp.zeros((n_ctx, HEAD_DIM), F32), sin_l], axis=0)
    return cos_t, sin_t


def _pack_gate_weights(wa, wi):
    n_dir, n_blocks, w, _ = wa.shape
    per_tile = MXU_TILE // w
    n_tiles = n_blocks // per_tile

    def dense(wx):
        wx = wx.reshape(n_dir, n_tiles, per_tile, w, w)
        eye = jnp.eye(per_tile, dtype=wx.dtype)
        full = jnp.einsum('dtpij,pq->dtpiqj', wx, eye)
        return full.reshape(n_dir, n_tiles, MXU_TILE, MXU_TILE)

    return jnp.concatenate([dense(wa), dense(wi)], axis=-1).astype(BF16)


def kernel(x, c, ctx, c_ctx, w_mod, b_mod, w_in, sink, conv_w, conv_b, rg_wa, rg_ba, rg_wi, rg_bi,
           rg_lambda, w_o_attn, w_o_rnn, w_out, ln1_g, ln1_b, router_w, router_bias, exp_w1, exp_w3,
           exp_w2, sh_w1, sh_w3, sh_w2, ln2_g, ln2_b):
    n_batch, n_lat, d = x.shape
    n_ctx = ctx.shape[1]
    n_layers = w_mod.shape[0]
    t = n_ctx + n_lat
    assert n_ctx % ROW_TILE == 0 and n_lat % ROW_TILE == 0 and n_batch + 1 <= MOD_ROWS
    assert exp_w1.shape[1:] == (N_EXPERTS, d, D_EXPERT) and sh_w1.shape[1:] == (d, D_EXPERT)
    alpha = (2 * n_layers) ** 0.25
    n_tiles = n_batch * t // ROW_TILE
    chunks_per_row_tile = EXPERT_ROW_TILE // CHUNK
    max_chunks = (TOP_K * n_batch * t // CHUNK + n_tiles * N_EXPERTS
                  + N_EXPERTS * (chunks_per_row_tile - 1))
    n_row_tiles = pl.cdiv(max_chunks, chunks_per_row_tile)

    xa = jnp.concatenate([ctx, x], axis=1)
    cc = jnp.zeros((MOD_ROWS, d), F32).at[:n_batch].set(c).at[n_batch].set(c_ctx)
    mod = _modulation(cc, w_mod, b_mod)
    cos_t, sin_t = _rope_tables(n_ctx, n_lat)
    mask_bias = _attn_mask_bias(n_ctx)

    for l in range(n_layers):
        mod_l = mod[l]
        q, k, v, xr, gy, sa, sr = _in_projection(xa, mod_l, w_in[l].astype(BF16), cos_t, sin_t, n_ctx)
        attn = _attention(sink[l], q, k, v, mask_bias, n_ctx)
        wg = _pack_gate_weights(rg_wa[l], rg_wi[l])
        hf, hb = _rnn_branch(xr, conv_w[l], conv_b[l][None, :], wg, rg_ba[l], rg_bi[l], rg_lambda[l],
                             n_ctx)
        rw_t = router_w[l].T
        rw_hi = rw_t.astype(BF16)
        rw_lo = (rw_t - rw_hi.astype(F32)).astype(BF16)
        x1, u2, logits_t = _out_projection(
            xa, attn, hf, hb, gy, sa, sr, mod_l, w_o_attn[l].astype(BF16), w_o_rnn[l].astype(BF16),
            w_out[l].astype(BF16), ln1_g[l][None, :], ln1_b[l][None, :], rw_hi, rw_lo, n_ctx, alpha)
        pos, wts, nch, lo = _route(logits_t, router_bias[l][:, None])
        tabs, tile_expert, n_valid = _dispatch_tables(nch[:, :, 0], lo[:, :, 0], n_row_tiles)
        xs = _dispatch(u2, pos, tabs, n_row_tiles)
        ys = _experts(xs.reshape(-1, d), exp_w1, exp_w3, exp_w2, l, tile_expert, n_valid, n_row_tiles)
        xa = _combine(ys.reshape(xs.shape), pos, wts, x1, u2, mod_l, sh_w1[l].astype(BF16), sh_w3[l].astype(BF16),
                      sh_w2[l].astype(BF16), ln2_g[l][None, :], ln2_b[l][None, :], tabs, n_ctx, alpha,
                      latent_only=(l == n_layers - 1))
    return xa
```
